```python
import math
import jax, jax.numpy as jnp
from jax import lax
import numpy as np

D_MODEL = 1024
BATCH = 8
SEQ = 2048
DEPTH = 2

D_PLE = 256
EPS = 1e-6
ATT_HEADS = 8
ATT_HD = 64
ATT_VD = 2 * ATT_HD
ATT_W = ATT_HEADS * ATT_VD
Q_BLOCK = 128
DN_HEADS = 4
DN_HD = 128
DN_W = DN_HEADS * DN_HD
DN_CONV = 4
DN_CHUNK = 64
GLA_HEADS = 4
GLA_KD = 64
GLA_VD = 128
GLA_KW = GLA_HEADS * GLA_KD
GLA_W = GLA_HEADS * GLA_VD
GLA_RANK = 16
GLA_TAU = 16.0
GLA_CHUNK = 64
D_MIX = ATT_W + DN_W + GLA_W

IN_SPLITS = (ATT_W, ATT_W, ATT_W, ATT_W,
             DN_W, DN_W, DN_W, DN_W,
             DN_HEADS, DN_HEADS,
             GLA_KW, GLA_KW, GLA_W, GLA_W,
             GLA_RANK)
D_IN = sum(IN_SPLITS)
SPLIT_IDX = tuple(int(i) for i in np.cumsum(IN_SPLITS)[:-1])

kernel_name = "hybrid_diffattn_gdn_gla_parallel"


def rmsnorm(x, g):
    xf = x.astype(jnp.float32)
    y = xf * lax.rsqrt(jnp.mean(xf * xf, axis=-1, keepdims=True) + EPS)
    return (y * g.astype(jnp.float32)).astype(x.dtype)


def l2norm(x):
    xf = x.astype(jnp.float32)
    return xf * lax.rsqrt(jnp.sum(xf * xf, axis=-1, keepdims=True) + EPS)


def heads(x, n):
    b, s, w = x.shape
    return x.reshape(b, s, n, w // n).transpose(0, 2, 1, 3)


def merge(x):
    b, n, s, d = x.shape
    return x.transpose(0, 2, 1, 3).reshape(b, s, n * d)


def alibi_slopes(n):
    return 2.0 ** (-8.0 * jnp.arange(1, n + 1, dtype=jnp.float32) / n)


def causal_conv(x, w):
    k, c = w.shape
    return lax.conv_general_dilated(
        x, w[:, None, :].astype(x.dtype), window_strides=(1,), padding=[(k - 1, 0)],
        dimension_numbers=('NWC', 'WIO', 'NWC'), feature_group_count=c)


def diff_attention(q, k, v, lam):
    s = q.shape[3]
    scale = ATT_HD ** -0.5
    slopes = alibi_slopes(ATT_HEADS)[:, None, None, None]
    kf = k.astype(jnp.float32)
    vf = v.astype(jnp.float32)
    outs = []
    for start in range(0, s, Q_BLOCK):
        end = min(start + Q_BLOCK, s)
        qb = q[:, :, :, start:end].astype(jnp.float32)
        sc = jnp.einsum('bhmqd,bhmkd->bhmqk', qb, kf[:, :, :, :end]) * scale
        dist = (jnp.arange(start, end)[:, None] - jnp.arange(end)[None, :]).astype(jnp.float32)
        sc = jnp.where(dist >= 0, sc - slopes * dist, -jnp.inf)
        pr = jax.nn.softmax(sc, axis=-1)
        a = pr[:, :, 0] - lam * pr[:, :, 1]
        outs.append(jnp.einsum('bhqk,bhkd->bhqd', a, vf[:, :, :end]))
    return jnp.concatenate(outs, axis=2)


def gated_delta_rule(q, k, v, beta, g):
    b, h, s, dk = q.shape
    dv = v.shape[-1]
    c = DN_CHUNK
    n = s // c
    q = q * dk ** -0.5
    rs = lambda t: t.reshape(b, h, n, c, *t.shape[3:])
    q, k, v, beta, g = rs(q), rs(k), rs(v), rs(beta), rs(g)
    gc = jnp.cumsum(g, axis=-1)
    tri_incl = jnp.tril(jnp.ones((c, c), bool))
    tri_strict = jnp.tril(jnp.ones((c, c), bool), -1)
    decay = jnp.exp(jnp.where(tri_incl, gc[..., :, None] - gc[..., None, :], -jnp.inf))
    kb = k * beta[..., None]
    m = jnp.where(tri_strict, jnp.einsum('bhnid,bhnjd->bhnij', kb, k) * decay, 0.0)
    eye = jnp.eye(c, dtype=jnp.float32)
    t_inv = lax.linalg.triangular_solve(eye + m, jnp.broadcast_to(eye, m.shape),
                                        left_side=True, lower=True, unit_diagonal=True)
    u = t_inv @ (v * beta[..., None])
    w = t_inv @ (kb * jnp.exp(gc)[..., None])
    a_intra = jnp.einsum('bhnid,bhnjd->bhnij', q, k) * decay
    g_last = gc[..., -1]
    k_dec = k * jnp.exp(g_last[..., None] - gc)[..., None]
    q_dec = q * jnp.exp(gc)[..., None]

    def step(state, xs):
        w_c, u_c, q_c, k_c, a_c, gl = xs
        v_new = u_c - jnp.einsum('bhcd,bhde->bhce', w_c, state)
        o = jnp.einsum('bhcd,bhde->bhce', q_c, state) + jnp.einsum('bhij,bhje->bhie', a_c, v_new)
        state = state * jnp.exp(gl)[..., None, None] + jnp.einsum('bhcd,bhce->bhde', k_c, v_new)
        return state, o

    xs = (jnp.moveaxis(w, 2, 0), jnp.moveaxis(u, 2, 0), jnp.moveaxis(q_dec, 2, 0),
          jnp.moveaxis(k_dec, 2, 0), jnp.moveaxis(a_intra, 2, 0), jnp.moveaxis(g_last, 2, 0))
    s0 = jnp.zeros((b, h, dk, dv), jnp.float32)
    _, o = lax.scan(step, s0, xs)
    return jnp.moveaxis(o, 0, 2).reshape(b, h, s, dv)


def gla_chunked(q, k, v, gk):
    b, h, s, dk = q.shape
    dv = v.shape[-1]
    c = GLA_CHUNK
    n = s // c
    q = q * dk ** -0.5
    rs = lambda t: jnp.moveaxis(t.reshape(b, h, n, c, t.shape[-1]), 2, 0)
    q, k, v, bc = rs(q), rs(k), rs(v), rs(gk)
    bc = jnp.cumsum(bc, axis=3)
    tri = jnp.tril(jnp.ones((c, c), bool))[..., None]

    def step(state, xs):
        q_c, k_c, v_c, b_c = xs
        inter = jnp.einsum('bhcd,bhde->bhce', q_c * jnp.exp(b_c), state)
        rel = jnp.exp(jnp.where(tri, b_c[:, :, :, None, :] - b_c[:, :, None, :, :], -jnp.inf))
        att = jnp.einsum('bhid,bhjd,bhijd->bhij', q_c, k_c, rel)
        o = inter + jnp.einsum('bhij,bhje->bhie', att, v_c)
        b_last = b_c[:, :, -1]
        state = state * jnp.exp(b_last)[..., None] + jnp.einsum(
            'bhcd,bhce->bhde', k_c * jnp.exp(b_last[:, :, None] - b_c), v_c)
        return state, o

    s0 = jnp.zeros((b, h, dk, dv), jnp.float32)
    _, o = lax.scan(step, s0, (q, k, v, bc))
    return jnp.moveaxis(o, 0, 2).reshape(b, h, s, dv)


def hybrid_layer(x, p_l, w_in, w_out, pre_g, post_g, lq1, lk1, lq2, lk2, att_subln,
                 dn_conv, dn_a_log, dn_dt_bias, dn_norm, gla_w2, gla_b, gla_norm,
                 ple_proj, ple_gate, ple_norm, layer_idx):
    dt = x.dtype
    bsz, s, _ = x.shape
    h = rmsnorm(x, pre_g)
    proj = h @ w_in
    (a_q, a_k, a_v, a_z, d_q, d_k, d_v, d_z, d_b, d_a,
     g_q, g_k, g_v, g_z, g_r) = jnp.split(proj, SPLIT_IDX, axis=-1)

    split2 = lambda t: t.reshape(bsz, s, ATT_HEADS, 2, ATT_HD).transpose(0, 2, 3, 1, 4)
    lam_init = 0.8 - 0.6 * math.exp(-0.3 * layer_idx)
    lam = (jnp.exp(jnp.sum(lq1.astype(jnp.float32) * lk1.astype(jnp.float32)))
           - jnp.exp(jnp.sum(lq2.astype(jnp.float32) * lk2.astype(jnp.float32))) + lam_init)
    o_att = diff_attention(split2(a_q), split2(a_k), heads(a_v, ATT_HEADS), lam)
    o_att = rmsnorm(o_att, att_subln) * (1.0 - lam_init)
    y_att = merge(o_att).astype(dt) * jax.nn.silu(a_z)

    qkv = jax.nn.silu(causal_conv(jnp.concatenate([d_q, d_k, d_v], axis=-1), dn_conv))
    c_q, c_k, c_v = jnp.split(qkv, 3, axis=-1)
    beta = jax.nn.sigmoid(d_b.astype(jnp.float32)).transpose(0, 2, 1)
    g = (-jnp.exp(dn_a_log.astype(jnp.float32))
         * jax.nn.softplus(d_a.astype(jnp.float32) + dn_dt_bias.astype(jnp.float32))).transpose(0, 2, 1)
    o_dn = gated_delta_rule(l2norm(heads(c_q, DN_HEADS)), l2norm(heads(c_k, DN_HEADS)),
                            heads(c_v, DN_HEADS).astype(jnp.float32), beta, g)
    y_dn = merge(rmsnorm(o_dn, dn_norm)).astype(dt) * jax.nn.silu(d_z)

    gk = jax.nn.log_sigmoid((g_r @ gla_w2 + gla_b).astype(jnp.float32)) / GLA_TAU
    o_gla = gla_chunked(heads(g_q, GLA_HEADS).astype(jnp.float32),
                        heads(g_k, GLA_HEADS).astype(jnp.float32),
                        heads(g_v, GLA_HEADS).astype(jnp.float32),
                        heads(gk, GLA_HEADS))
    y_gla = merge(rmsnorm(o_gla, gla_norm)).astype(dt) * jax.nn.silu(g_z)

    y = jnp.concatenate([y_att, y_dn, y_gla], axis=-1) @ w_out
    x = x + rmsnorm(y, post_g)

    gate = jax.nn.sigmoid(x @ ple_gate)
    x = x + rmsnorm((p_l @ ple_proj) * gate, ple_norm)
    return x


def setup_inputs(seed: int = 0) -> dict:
    key = jax.random.key(seed)
    ks = jax.random.split(key, 24)
    nrm = lambda k, shape, scale: jax.random.normal(k, shape, jnp.float32) * scale
    gain = lambda k, shape: 1.0 + 0.02 * jax.random.normal(k, shape, jnp.float32)
    dt_init = jnp.exp(jax.random.uniform(ks[13], (DEPTH, DN_HEADS), jnp.float32,
                                         math.log(1e-3), math.log(1e-1)))
    return {
        "x": nrm(ks[0], (BATCH, SEQ, D_MODEL), 1.0),
        "p": nrm(ks[1], (DEPTH, BATCH, SEQ, D_PLE), 1.0),
        "w_in": nrm(ks[2], (DEPTH, D_MODEL, D_IN), D_MODEL ** -0.5),
        "w_out": nrm(ks[3], (DEPTH, D_MIX, D_MODEL), D_MIX ** -0.5),
        "pre_gain": gain(ks[4], (DEPTH, D_MODEL)),
        "post_gain": gain(ks[5], (DEPTH, D_MODEL)),
        "att_lq1": nrm(ks[6], (DEPTH, ATT_HD), 0.1),
        "att_lk1": nrm(ks[7], (DEPTH, ATT_HD), 0.1),
        "att_lq2": nrm(ks[8], (DEPTH, ATT_HD), 0.1),
        "att_lk2": nrm(ks[9], (DEPTH, ATT_HD), 0.1),
        "att_subln": gain(ks[10], (DEPTH, ATT_VD)),
        "dn_conv": nrm(ks[11], (DEPTH, DN_CONV, 3 * DN_W), DN_CONV ** -0.5),
        "dn_a_log": jnp.log(jax.random.uniform(ks[12], (DEPTH, DN_HEADS), jnp.float32, 1.0, 16.0)),
        "dn_dt_bias": dt_init + jnp.log(-jnp.expm1(-dt_init)),
        "dn_norm": gain(ks[14], (DEPTH, DN_HD)),
        "gla_w2": nrm(ks[15], (DEPTH, GLA_RANK, GLA_KW), GLA_RANK ** -0.5),
        "gla_b": nrm(ks[16], (DEPTH, GLA_KW), 0.01),
        "gla_norm": gain(ks[17], (DEPTH, GLA_VD)),
        "ple_proj": nrm(ks[18], (DEPTH, D_PLE, D_MODEL), D_PLE ** -0.5),
        "ple_gate": nrm(ks[19], (DEPTH, D_MODEL, D_MODEL), D_MODEL ** -0.5),
        "ple_norm": gain(ks[20], (DEPTH, D_MODEL)),
    }


def reference(x, p, w_in, w_out, pre_gain, post_gain, att_lq1, att_lk1, att_lq2, att_lk2,
              att_subln, dn_conv, dn_a_log, dn_dt_bias, dn_norm, gla_w2, gla_b, gla_norm,
              ple_proj, ple_gate, ple_norm):
    for i in range(DEPTH):
        x = hybrid_layer(x, p[i], w_in[i], w_out[i], pre_gain[i], post_gain[i],
                         att_lq1[i], att_lk1[i], att_lq2[i], att_lk2[i], att_subln[i],
                         dn_conv[i], dn_a_log[i], dn_dt_bias[i], dn_norm[i],
                         gla_w2[i], gla_b[i], gla_norm[i],
                         ple_proj[i], ple_gate[i], ple_norm[i], i)
    return x
```

```python
import functools
import math

import numpy as np
import jax
import jax.numpy as jnp
from jax import lax
from jax.experimental import pallas as pl
from jax.experimental.pallas import tpu as pltpu

F32 = jnp.float32
BF16 = jnp.bfloat16

D_MODEL = 1024
D_PLE = 256
EPS = 1e-6
ATT_HEADS = 8
ATT_HD = 64
ATT_VD = 128
ATT_W = 1024
DN_HEADS = 4
DN_HD = 128
DN_W = 512
DN_CONV = 4
GLA_HEADS = 4
GLA_KD = 64
GLA_VD = 128
GLA_KW = 256
GLA_W = 512
GLA_RANK = 16
GLA_TAU = 16.0
CHUNK = 64
D_MIX = 2048
D_MAIN = 7680
D_SMALL = 128
LANES = 128
VMEM_LIMIT = 48 * 1024 * 1024

NT = (((1,), (1,)), ((), ()))
TN = (((0,), (0,)), ((), ()))
GLA_LEVELS = (32, 16, 8, 4, 2, 1)


def _sigmoid(x):
    return 1.0 / (1.0 + jnp.exp(-x))


def _softplus(x):
    return jnp.maximum(x, 0.0) + jnp.log(1.0 + jnp.exp(-jnp.abs(x)))


def _split_bf16(x):
    hi = x.astype(BF16)
    lo = (x - hi.astype(F32)).astype(BF16)
    return hi, lo


def _proj_body(x_ref, g_ref, wm_ref, ws_ref, wst_ref, pm_ref, ps_ref, pst_ref, h_scr):
    @pl.when(pl.program_id(1) == 0)
    def _():
        x = x_ref[...]
        h = (x * lax.rsqrt(jnp.mean(x * x, axis=-1, keepdims=True) + EPS) * g_ref[...]).astype(BF16)
        h_scr[...] = h
        ps_ref[...] = jnp.dot(h, ws_ref[...], preferred_element_type=F32)
        pst_ref[...] = lax.dot_general(wst_ref[...], h, NT, preferred_element_type=F32)

    pm_ref[...] = jnp.dot(h_scr[...], wm_ref[...], preferred_element_type=F32).astype(BF16)


def _proj(x2, g, wm, ws, wst, *, tm, tn):
    t = x2.shape[0]
    return pl.pallas_call(
        _proj_body,
        grid=(t // tm, D_MAIN // tn),
        in_specs=[
            pl.BlockSpec((tm, D_MODEL), lambda i, j: (i, 0)),
            pl.BlockSpec((1, D_MODEL), lambda i, j: (0, 0)),
            pl.BlockSpec((D_MODEL, tn), lambda i, j: (0, j)),
            pl.BlockSpec((D_MODEL, D_SMALL), lambda i, j: (0, 0)),
            pl.BlockSpec((8, D_MODEL), lambda i, j: (0, 0)),
        ],
        out_specs=[
            pl.BlockSpec((tm, tn), lambda i, j: (i, j)),
            pl.BlockSpec((tm, D_SMALL), lambda i, j: (i, 0)),
            pl.BlockSpec((8, tm), lambda i, j: (0, i)),
        ],
        out_shape=[
            jax.ShapeDtypeStruct((t, D_MAIN), BF16),
            jax.ShapeDtypeStruct((t, D_SMALL), F32),
            jax.ShapeDtypeStruct((8, t), F32),
        ],
        scratch_shapes=[pltpu.VMEM((tm, D_MODEL), BF16)],
        compiler_params=pltpu.CompilerParams(
            dimension_semantics=("parallel", "arbitrary"), vmem_limit_bytes=VMEM_LIMIT),
        name="proj",
    )(x2, g, wm, ws, wst)


def _attn_body(q_ref, k_ref, v_ref, z_ref, lqk_ref, subln_ref, o_ref, acc_scr, *, tq, lam_init):
    h = pl.program_id(1)
    qi = pl.program_id(2)
    q = q_ref[0]
    lane = lax.broadcasted_iota(jnp.int32, (tq, LANES), 1)
    zero = jnp.zeros_like(q)
    scale = jnp.asarray(ATT_HD ** -0.5, BF16)
    qs = jnp.concatenate([jnp.where(lane < ATT_HD, q, zero),
                          jnp.where(lane >= ATT_HD, q, zero)], axis=0) * scale
    slope = jnp.exp2(jnp.full((1, tq), -8.0 / ATT_HEADS, F32) * (h + 1).astype(F32))
    kio = lax.broadcasted_iota(jnp.int32, (1, tq), 1)
    acc_scr[...] = jnp.zeros_like(acc_scr)

    def tile(j, m, l, masked):
        off = pl.multiple_of(j * tq, tq)
        k = k_ref[0, pl.ds(off, tq), :]
        v = v_ref[0, pl.ds(off, tq), :]
        s = lax.dot_general(qs, k, NT, preferred_element_type=F32)
        s = s + slope * ((j - qi) * tq + kio).astype(F32)
        if masked:
            row = lax.broadcasted_iota(jnp.int32, (2 * tq, tq), 0)
            col = lax.broadcasted_iota(jnp.int32, (2 * tq, tq), 1)
            s = jnp.where(col <= jnp.where(row >= tq, row - tq, row), s, -1e30)
        m_new = jnp.maximum(m, jnp.max(s, axis=-1, keepdims=True))
        alpha = jnp.exp(m - m_new)
        p = jnp.exp(s - m_new)
        l_new = alpha * l + jnp.sum(p, axis=-1, keepdims=True)
        acc_scr[...] = alpha * acc_scr[...] + jnp.dot(p.astype(BF16), v, preferred_element_type=F32)
        return m_new, l_new

    m0 = jnp.full((2 * tq, 1), -1e30, F32)
    l0 = jnp.zeros((2 * tq, 1), F32)
    m, l = lax.fori_loop(0, qi, lambda j, c: tile(j, c[0], c[1], False), (m0, l0))
    m, l = tile(qi, m, l, True)

    o = acc_scr[...] * (1.0 / l)
    lqk = lqk_ref[...]
    lam = (jnp.exp(jnp.sum(lqk[0:1] * lqk[1:2], axis=-1, keepdims=True))
           - jnp.exp(jnp.sum(lqk[2:3] * lqk[3:4], axis=-1, keepdims=True)) + lam_init)
    o = o[:tq] - lam * o[tq:]
    o = o * lax.rsqrt(jnp.mean(o * o, axis=-1, keepdims=True) + EPS) * subln_ref[...] * (1.0 - lam_init)
    z = z_ref[0].astype(F32)
    o_ref[0] = (o * (z * _sigmoid(z))).astype(BF16)


def _attn(pm3, lqk, subln, *, tq, lam_init):
    b, s, _ = pm3.shape
    nq = ATT_W // LANES
    return pl.pallas_call(
        functools.partial(_attn_body, tq=tq, lam_init=lam_init),
        grid=(b, ATT_HEADS, s // tq),
        in_specs=[
            pl.BlockSpec((1, tq, LANES), lambda bi, h, qi: (bi, qi, h)),
            pl.BlockSpec((1, s, LANES), lambda bi, h, qi: (bi, 0, nq + h)),
            pl.BlockSpec((1, s, LANES), lambda bi, h, qi: (bi, 0, 2 * nq + h)),
            pl.BlockSpec((1, tq, LANES), lambda bi, h, qi: (bi, qi, 3 * nq + h)),
            pl.BlockSpec((4, ATT_HD), lambda bi, h, qi: (0, 0)),
            pl.BlockSpec((1, ATT_VD), lambda bi, h, qi: (0, 0)),
        ],
        out_specs=pl.BlockSpec((1, tq, LANES), lambda bi, h, qi: (bi, qi, h)),
        out_shape=jax.ShapeDtypeStruct((b, s, ATT_W), BF16),
        scratch_shapes=[pltpu.VMEM((2 * tq, ATT_VD), F32)],
        compiler_params=pltpu.CompilerParams(
            dimension_semantics=("parallel", "parallel", "arbitrary"), vmem_limit_bytes=VMEM_LIMIT),
        name="attn",
    )(pm3, pm3, pm3, pm3, lqk, subln)


def _dn_body(dq_ref, dk_ref, dv_ref, dz_ref, ps_ref, rows_ref, cw_ref, acol_ref, dtcol_ref,
             arow_ref, dtrow_ref, norm_ref, ltri_ref, utri_ref, o_ref, st_scr, *, n_chunks):
    c_ = CHUNK
    st_scr[...] = jnp.zeros_like(st_scr)
    ii = lax.broadcasted_iota(jnp.int32, (c_, c_), 0)
    jj = lax.broadcasted_iota(jnp.int32, (c_, c_), 1)
    tri_incl = ii >= jj
    tri_strict = ii > jj
    eye = ii == jj
    cw = cw_ref[...]
    pad = 16

    def conv_silu(win, w4):
        y = (win[pad - 3:pad - 3 + c_] * w4[0:1] + win[pad - 2:pad - 2 + c_] * w4[1:2]
             + win[pad - 1:pad - 1 + c_] * w4[2:3] + win[pad:pad + c_] * w4[3:4])
        return y * _sigmoid(y)

    def window(ref, r0, first):
        if first:
            return jnp.concatenate([jnp.zeros((pad, DN_W), F32), ref[0, 0:c_, :].astype(F32)], axis=0)
        return ref[0, pl.ds(pl.multiple_of(r0 - pad, pad), c_ + pad), :].astype(F32)

    def chunk(c, first):
        r0 = 0 if first else pl.multiple_of(c * c_, c_)
        rows = pl.ds(r0, c_)
        cq = conv_silu(window(dq_ref, r0, first), cw[:, 0:DN_W])
        ck = conv_silu(window(dk_ref, r0, first), cw[:, DN_W:2 * DN_W])
        cv = conv_silu(window(dv_ref, r0, first), cw[:, 2 * DN_W:3 * DN_W])
        ps = ps_ref[rows, :]
        beta_all = _sigmoid(ps)
        g_all = -jnp.exp(acol_ref[...]) * _softplus(ps + dtcol_ref[...])
        g_hi, g_lo = _split_bf16(g_all)
        gc_all = jnp.dot(ltri_ref[...], jnp.concatenate([g_hi, g_lo], axis=0),
                         preferred_element_type=F32)
        rw = rows_ref[0, c]
        g_row = -jnp.exp(arow_ref[...]) * _softplus(rw + dtrow_ref[...])
        r_hi, r_lo = _split_bf16(g_row)
        gc_rows = jnp.dot(jnp.concatenate([r_hi, r_lo], axis=1), utri_ref[...],
                          preferred_element_type=F32)
        for h in range(DN_HEADS):
            sl = slice(h * DN_HD, (h + 1) * DN_HD)
            qh, kh, vh = cq[:, sl], ck[:, sl], cv[:, sl]
            qn = qh * (lax.rsqrt(jnp.sum(qh * qh, axis=-1, keepdims=True) + EPS) * DN_HD ** -0.5)
            kn = kh * lax.rsqrt(jnp.sum(kh * kh, axis=-1, keepdims=True) + EPS)
            beta = beta_all[:, h:h + 1]
            gcc = gc_all[:, 4 + h:5 + h]
            gcr = gc_rows[4 + h:5 + h, :]
            gl = gcr[:, c_ - 1:c_]
            decay = jnp.exp(jnp.where(tri_incl, gcc - gcr, -1e30))
            kb = kn * beta
            gram = lax.dot_general(jnp.concatenate([kb, qn], axis=0).astype(BF16), kn.astype(BF16),
                                   NT, preferred_element_type=F32)
            x = jnp.where(tri_strict, -(gram[:c_] * decay), 0.0)
            a_intra = gram[c_:] * decay
            acc = jnp.where(eye, 1.0, x)
            xp = jnp.dot(x.astype(BF16), x.astype(BF16), preferred_element_type=F32)
            for _ in range(4):
                both = jnp.dot(jnp.concatenate([acc, xp], axis=0).astype(BF16), xp.astype(BF16),
                               preferred_element_type=F32)
                acc = acc + both[:c_]
                xp = both[c_:]
            t_inv = acc + jnp.dot(acc.astype(BF16), xp.astype(BF16), preferred_element_type=F32)
            gexp = jnp.exp(gcc)
            uw = jnp.dot(t_inv.astype(BF16),
                         jnp.concatenate([vh * beta, kb * gexp], axis=1).astype(BF16),
                         preferred_element_type=F32)
            u, w = uw[:, :DN_HD], uw[:, DN_HD:]
            q_dec = qn * gexp
            k_dec = kn * jnp.exp(gl - gcc)
            state = st_scr[h]
            wq = jnp.dot(jnp.concatenate([w, q_dec], axis=0).astype(BF16), state.astype(BF16),
                         preferred_element_type=F32)
            v_new = u - wq[:c_]
            o = wq[c_:] + jnp.dot(a_intra.astype(BF16), v_new.astype(BF16), preferred_element_type=F32)
            st_scr[h] = state * jnp.exp(gl) + lax.dot_general(
                k_dec.astype(BF16), v_new.astype(BF16), TN, preferred_element_type=F32)
            o = o * lax.rsqrt(jnp.mean(o * o, axis=-1, keepdims=True) + EPS) * norm_ref[...]
            z = dz_ref[0, rows, sl].astype(F32)
            o_ref[0, rows, sl] = (o * (z * _sigmoid(z))).astype(BF16)

    chunk(0, True)

    def body(c, carry):
        chunk(c, False)
        return carry

    lax.fori_loop(1, n_chunks, body, 0)


def _deltanet(pm3, ps, rows4, cw, acol, dtcol, arow, dtrow, norm, ltri, utri):
    b, s, _ = pm3.shape
    n_chunks = s // CHUNK
    base = 4 * ATT_W // DN_W
    full = lambda shape: pl.BlockSpec(shape, lambda bi: (0,) * len(shape))
    return pl.pallas_call(
        functools.partial(_dn_body, n_chunks=n_chunks),
        grid=(b,),
        in_specs=[
            pl.BlockSpec((1, s, DN_W), lambda bi: (bi, 0, base)),
            pl.BlockSpec((1, s, DN_W), lambda bi: (bi, 0, base + 1)),
            pl.BlockSpec((1, s, DN_W), lambda bi: (bi, 0, base + 2)),
            pl.BlockSpec((1, s, DN_W), lambda bi: (bi, 0, base + 3)),
            pl.BlockSpec((s, D_SMALL), lambda bi: (bi, 0)),
            pl.BlockSpec((1, n_chunks, 8, CHUNK), lambda bi: (bi, 0, 0, 0)),
            full((DN_CONV, 3 * DN_W)),
            full((1, D_SMALL)), full((1, D_SMALL)), full((8, 1)), full((8, 1)),
            full((1, DN_HD)),
            full((CHUNK, 2 * CHUNK)), full((2 * CHUNK, CHUNK)),
        ],
        out_specs=pl.BlockSpec((1, s, DN_W), lambda bi: (bi, 0, 0)),
        out_shape=jax.ShapeDtypeStruct((b, s, DN_W), BF16),
        scratch_shapes=[pltpu.VMEM((DN_HEADS, DN_HD, DN_HD), F32)],
        compiler_params=pltpu.CompilerParams(
            dimension_semantics=("parallel",), vmem_limit_bytes=VMEM_LIMIT),
        name="deltanet",
    )(pm3, pm3, pm3, pm3, ps, rows4, cw, acol, dtcol, arow, dtrow, norm, ltri, utri)


def _gla_body(gq_ref, gk_ref, gv_ref, gz_ref, ps_ref, w2_ref, b2_ref, norm_ref, cmat_ref,
              o_ref, st_scr, *, n_chunks):
    c_ = CHUNK
    st_scr[...] = jnp.zeros_like(st_scr)
    rowi = lax.broadcasted_iota(jnp.int32, (c_, 1), 0)
    upper = [(rowi % (2 * s)) >= s for s in GLA_LEVELS]
    ii = lax.broadcasted_iota(jnp.int32, (2 * c_, c_), 0) % c_
    jj = lax.broadcasted_iota(jnp.int32, (2 * c_, c_), 1)
    same = [(ii // (2 * s)) == (jj // (2 * s)) for s in GLA_LEVELS]
    eye = ii == jj
    lane = lax.broadcasted_iota(jnp.int32, (c_, LANES), 1)
    first_head = lane < GLA_KD
    lane_sq = lax.broadcasted_iota(jnp.int32, (GLA_VD, LANES), 1)
    head_lanes = [lane_sq < GLA_KD, lane_sq >= GLA_KD]

    def stack_heads(t):
        return jnp.concatenate([jnp.where(first_head, t, 0.0), jnp.where(first_head, 0.0, t)], axis=0)

    def body(c, carry):
        r0 = pl.multiple_of(c * c_, c_)
        rows = pl.ds(r0, c_)
        pre = jnp.dot(ps_ref[rows, :].astype(BF16), w2_ref[...], preferred_element_type=F32) + b2_ref[...]
        gk = (jnp.minimum(pre, 0.0) - jnp.log(1.0 + jnp.exp(-jnp.abs(pre)))) * (1.0 / GLA_TAU)
        g_hi, g_lo = _split_bf16(gk)
        br = jnp.dot(cmat_ref[...], jnp.concatenate([g_hi, g_lo], axis=0), preferred_element_type=F32)
        for p in range(GLA_HEADS // 2):
            pls = slice(p * LANES, (p + 1) * LANES)
            bcum = br[0:c_, pls]
            q = gq_ref[0, rows, pls].astype(F32) * (GLA_KD ** -0.5)
            k = gk_ref[0, rows, pls].astype(F32)
            kb16 = k.astype(BF16)
            att = jnp.where(eye, lax.dot_general(stack_heads(q).astype(BF16), kb16, NT,
                                                 preferred_element_type=F32), 0.0)
            for li in range(len(GLA_LEVELS)):
                ref = br[c_ * (li + 1):c_ * (li + 2), pls]
                up = upper[li]
                e = jnp.exp(jnp.where(up, bcum - ref, ref - bcum))
                qt = jnp.where(up, q * e, 0.0)
                kt = jnp.where(up, 0.0, k * e)
                part = lax.dot_general(stack_heads(qt).astype(BF16), kt.astype(BF16), NT,
                                       preferred_element_type=F32)
                att = att + jnp.where(same[li], part, 0.0)
            blast = bcum[c_ - 1:c_, :]
            qe = (q * jnp.exp(bcum)).astype(BF16)
            ke = (k * jnp.exp(blast - bcum)).astype(BF16)
            dec = jnp.exp(blast)
            for hh in range(2):
                h = 2 * p + hh
                hs = slice(h * GLA_VD, (h + 1) * GLA_VD)
                vh = gv_ref[0, rows, hs]
                state = st_scr[h]
                o = (lax.dot_general(qe, state.astype(BF16), NT, preferred_element_type=F32)
                     + jnp.dot(att[hh * c_:(hh + 1) * c_].astype(BF16), vh, preferred_element_type=F32))
                upd = lax.dot_general(vh, ke, TN, preferred_element_type=F32)
                st_scr[h] = state * dec + jnp.where(head_lanes[hh], upd, 0.0)
                o = o * lax.rsqrt(jnp.mean(o * o, axis=-1, keepdims=True) + EPS) * norm_ref[...]
                z = gz_ref[0, rows, hs].astype(F32)
                o_ref[0, rows, hs] = (o * (z * _sigmoid(z))).astype(BF16)
        return carry

    lax.fori_loop(0, n_chunks, body, 0)


def _gla(pm3, ps, w2, b2, norm, cmat):
    b, s, _ = pm3.shape
    n_chunks = s // CHUNK
    off = 4 * ATT_W + 4 * DN_W
    full = lambda shape: pl.BlockSpec(shape, lambda bi: (0,) * len(shape))
    return pl.pallas_call(
        functools.partial(_gla_body, n_chunks=n_chunks),
        grid=(b,),
        in_specs=[
            pl.BlockSpec((1, s, GLA_KW), lambda bi: (bi, 0, off // GLA_KW)),
            pl.BlockSpec((1, s, GLA_KW), lambda bi: (bi, 0, off // GLA_KW + 1)),
            pl.BlockSpec((1, s, GLA_W), lambda bi: (bi, 0, (off + 2 * GLA_KW) // GLA_W)),
            pl.BlockSpec((1, s, GLA_W), lambda bi: (bi, 0, (off + 2 * GLA_KW) // GLA_W + 1)),
            pl.BlockSpec((s, D_SMALL), lambda bi: (bi, 0)),
            full((D_SMALL, GLA_KW)), full((1, GLA_KW)), full((1, GLA_VD)),
            full(((len(GLA_LEVELS) + 1) * CHUNK, 2 * CHUNK)),
        ],
        out_specs=pl.BlockSpec((1, s, GLA_W), lambda bi: (bi, 0, 0)),
        out_shape=jax.ShapeDtypeStruct((b, s, GLA_W), BF16),
        scratch_shapes=[pltpu.VMEM((GLA_HEADS, GLA_VD, LANES), F32)],
        compiler_params=pltpu.CompilerParams(
            dimension_semantics=("parallel",), vmem_limit_bytes=VMEM_LIMIT),
        name="gla",
    )(pm3, pm3, pm3, pm3, ps, w2, b2, norm, cmat)


def _out_body(x_ref, ya_ref, yd_ref, yg_ref, p_ref, wo_ref, postg_ref, pg_ref, pp_ref, pn_ref, o_ref):
    y = (jnp.dot(ya_ref[...], wo_ref[0:ATT_W, :], preferred_element_type=F32)
         + jnp.dot(yd_ref[...], wo_ref[ATT_W:ATT_W + DN_W, :], preferred_element_type=F32)
         + jnp.dot(yg_ref[...], wo_ref[ATT_W + DN_W:D_MIX, :], preferred_element_type=F32))
    x1 = x_ref[...] + y * lax.rsqrt(jnp.mean(y * y, axis=-1, keepdims=True) + EPS) * postg_ref[...]
    gate = _sigmoid(jnp.dot(x1.astype(BF16), pg_ref[...], preferred_element_type=F32))
    pe = jnp.dot(p_ref[...].astype(BF16), pp_ref[...], preferred_element_type=F32) * gate
    o_ref[...] = x1 + pe * lax.rsqrt(jnp.mean(pe * pe, axis=-1, keepdims=True) + EPS) * pn_ref[...]


def _out(x2, ya, yd, yg, p2, wo, postg, pg, pp, pn, *, tm):
    t = x2.shape[0]
    tile = lambda w: pl.BlockSpec((tm, w), lambda i: (i, 0))
    full = lambda shape: pl.BlockSpec(shape, lambda i: (0, 0))
    return pl.pallas_call(
        _out_body,
        grid=(t // tm,),
        in_specs=[tile(D_MODEL), tile(ATT_W), tile(DN_W), tile(GLA_W), tile(D_PLE),
                  full((D_MIX, D_MODEL)), full((1, D_MODEL)), full((D_MODEL, D_MODEL)),
                  full((D_PLE, D_MODEL)), full((1, D_MODEL))],
        out_specs=tile(D_MODEL),
        out_shape=jax.ShapeDtypeStruct((t, D_MODEL), F32),
        compiler_params=pltpu.CompilerParams(
            dimension_semantics=("parallel",), vmem_limit_bytes=VMEM_LIMIT),
        name="out",
    )(x2, ya, yd, yg, p2, wo, postg, pg, pp, pn)


def _chunk_constants():
    c = CHUNK
    ltri = np.tril(np.ones((c, c), np.float32))
    blocks = [ltri]
    idx = np.arange(c)
    for s in GLA_LEVELS:
        bnd = (idx // (2 * s)) * (2 * s) + s - 1
        blocks.append(ltri[bnd])
    cmat = np.concatenate(blocks, axis=0)
    return (jnp.asarray(np.concatenate([ltri, ltri], axis=1), BF16),
            jnp.asarray(np.concatenate([ltri.T, ltri.T], axis=0), BF16),
            jnp.asarray(np.concatenate([cmat, cmat], axis=1), BF16))


def _pick(n, pref):
    for t in pref:
        if n % t == 0:
            return t
    return n


def _layer(x, p_l, w_in, w_out, pre_g, post_g, lq1, lk1, lq2, lk2, att_subln, dn_conv, dn_a_log,
           dn_dt_bias, dn_norm, gla_w2, gla_b, gla_norm, ple_proj, ple_gate, ple_norm, layer_idx, consts):
    b, s, _ = x.shape
    t = b * s
    ltri, utri, cmat = consts
    lam_init = 0.8 - 0.6 * math.exp(-0.3 * layer_idx)

    n_big = 4 * ATT_W + 4 * DN_W
    n_small = 2 * DN_HEADS
    wm = jnp.concatenate([w_in[:, :n_big], w_in[:, n_big + n_small:n_big + n_small + 2 * GLA_KW + 2 * GLA_W]],
                         axis=1).astype(BF16)
    w_small = jnp.concatenate([w_in[:, n_big:n_big + n_small], w_in[:, -GLA_RANK:]], axis=1)
    ws = jnp.pad(w_small, ((0, 0), (0, D_SMALL - w_small.shape[1]))).astype(BF16)
    wst = w_small[:, :n_small].T.astype(BF16)

    x2 = x.reshape(t, D_MODEL)
    pm, ps, pst = _proj(x2, pre_g.reshape(1, D_MODEL), wm, ws, wst,
                        tm=_pick(t, (1024, 512, 256)), tn=_pick(D_MAIN, (1536, 1280, 768)))
    pm3 = pm.reshape(b, s, D_MAIN)

    lqk = jnp.stack([lq1, lk1, lq2, lk2], axis=0)
    y_att = _attn(pm3, lqk, att_subln.reshape(1, ATT_VD), tq=_pick(s, (256, 128)), lam_init=lam_init)

    rows4 = pst.reshape(8, b, s // CHUNK, CHUNK).transpose(1, 2, 0, 3)
    lane_vec = lambda v, off: jnp.zeros((1, D_SMALL), F32).at[0, off:off + v.shape[0]].set(v)
    row_vec = lambda v, off: jnp.zeros((8, 1), F32).at[off:off + v.shape[0], 0].set(v)
    y_dn = _deltanet(pm3, ps, rows4, dn_conv, lane_vec(dn_a_log, DN_HEADS), lane_vec(dn_dt_bias, DN_HEADS),
                     row_vec(dn_a_log, DN_HEADS), row_vec(dn_dt_bias, DN_HEADS),
                     dn_norm.reshape(1, DN_HD), ltri, utri)

    w2 = jnp.zeros((D_SMALL, GLA_KW), F32).at[n_small:n_small + GLA_RANK].set(gla_w2).astype(BF16)
    y_gla = _gla(pm3, ps, w2, gla_b.reshape(1, GLA_KW), gla_norm.reshape(1, GLA_VD), cmat)

    out = _out(x2, y_att.reshape(t, ATT_W), y_dn.reshape(t, DN_W), y_gla.reshape(t, GLA_W),
               p_l.reshape(t, D_PLE), w_out.astype(BF16), post_g.reshape(1, D_MODEL),
               ple_gate.astype(BF16), ple_proj.astype(BF16), ple_norm.reshape(1, D_MODEL),
               tm=_pick(t, (512, 256)))
    return out.reshape(b, s, D_MODEL)


def kernel(x, p, w_in, w_out, pre_gain, post_gain, att_lq1, att_lk1, att_lq2, att_lk2, att_subln, dn_conv, dn_a_log, dn_dt_bias, dn_norm, gla_w2, gla_b, gla_norm, ple_proj, ple_gate, ple_norm):
    consts = _chunk_constants()
    for i in range(p.shape[0]):
        x = _layer(x, p[i], w_in[i], w_out[i], pre_gain[i], post_gain[i], att_lq1[i], att_lk1[i],
                   att_lq2[i], att_lk2[i], att_subln[i], dn_conv[i], dn_a_log[i], dn_dt_bias[i],
                   dn_norm[i], gla_w2[i], gla_b[i], gla_norm[i], ple_proj[i], ple_gate[i],
                   ple_norm[i], i, consts)
    return x
```

```python
import functools
import math

import numpy as np
import jax
import jax.numpy as jnp
from jax import lax
from jax.experimental import pallas as pl
from jax.experimental.pallas import tpu as pltpu

F32 = jnp.float32
BF16 = jnp.bfloat16

D_MODEL = 1024
D_PLE = 256
EPS = 1e-6
ATT_HEADS = 8
ATT_HD = 64
ATT_VD = 128
ATT_W = 1024
DN_HEADS = 4
DN_HD = 128
DN_W = 512
DN_CONV = 4
GLA_HEADS = 4
GLA_KD = 64
GLA_VD = 128
GLA_KW = 256
GLA_W = 512
GLA_RANK = 16
GLA_TAU = 16.0
CHUNK = 64
D_MIX = 2048
D_MAIN = 7680
D_SMALL = 128
LANES = 128
VMEM_LIMIT = 48 * 1024 * 1024

NT = (((1,), (1,)), ((), ()))
TN = (((0,), (0,)), ((), ()))
GLA_LEVELS = (32, 16, 8, 4, 2, 1)


def _sigmoid(x):
    return 1.0 / (1.0 + jnp.exp(-x))


def _softplus(x):
    return jnp.maximum(x, 0.0) + jnp.log(1.0 + jnp.exp(-jnp.abs(x)))


def _split_bf16(x):
    hi = x.astype(BF16)
    lo = (x - hi.astype(F32)).astype(BF16)
    return hi, lo


def _proj_body(x_ref, g_ref, wm_ref, ws_ref, wst_ref, pm_ref, ps_ref, pst_ref, h_scr):
    @pl.when(pl.program_id(1) == 0)
    def _():
        x = x_ref[...]
        h = (x * lax.rsqrt(jnp.mean(x * x, axis=-1, keepdims=True) + EPS) * g_ref[...]).astype(BF16)
        h_scr[...] = h
        ps_ref[...] = jnp.dot(h, ws_ref[...], preferred_element_type=F32)
        pst_ref[...] = lax.dot_general(wst_ref[...], h, NT, preferred_element_type=F32)

    pm_ref[...] = jnp.dot(h_scr[...], wm_ref[...], preferred_element_type=F32).astype(BF16)


def _proj(x2, g, wm, ws, wst, *, tm, tn):
    t = x2.shape[0]
    return pl.pallas_call(
        _proj_body,
        grid=(t // tm, D_MAIN // tn),
        in_specs=[
            pl.BlockSpec((tm, D_MODEL), lambda i, j: (i, 0)),
            pl.BlockSpec((1, D_MODEL), lambda i, j: (0, 0)),
            pl.BlockSpec((D_MODEL, tn), lambda i, j: (0, j)),
            pl.BlockSpec((D_MODEL, D_SMALL), lambda i, j: (0, 0)),
            pl.BlockSpec((8, D_MODEL), lambda i, j: (0, 0)),
        ],
        out_specs=[
            pl.BlockSpec((tm, tn), lambda i, j: (i, j)),
            pl.BlockSpec((tm, D_SMALL), lambda i, j: (i, 0)),
            pl.BlockSpec((8, tm), lambda i, j: (0, i)),
        ],
        out_shape=[
            jax.ShapeDtypeStruct((t, D_MAIN), BF16),
            jax.ShapeDtypeStruct((t, D_SMALL), F32),
            jax.ShapeDtypeStruct((8, t), F32),
        ],
        scratch_shapes=[pltpu.VMEM((tm, D_MODEL), BF16)],
        compiler_params=pltpu.CompilerParams(
            dimension_semantics=("parallel", "arbitrary"), vmem_limit_bytes=VMEM_LIMIT),
        name="proj",
    )(x2, g, wm, ws, wst)


def _attn_body(q_ref, k_ref, v_ref, z_ref, lqk_ref, subln_ref, o_ref, acc_scr, *, tq, lam_init):
    h = pl.program_id(1)
    qi = pl.program_id(2)
    q = q_ref[0]
    lane = lax.broadcasted_iota(jnp.int32, (tq, LANES), 1)
    zero = jnp.zeros_like(q)
    scale = jnp.asarray(ATT_HD ** -0.5, BF16)
    qs = jnp.concatenate([jnp.where(lane < ATT_HD, q, zero),
                          jnp.where(lane >= ATT_HD, q, zero)], axis=0) * scale
    slope = jnp.exp2(jnp.full((1, tq), -8.0 / ATT_HEADS, F32) * (h + 1).astype(F32))
    kio = lax.broadcasted_iota(jnp.int32, (1, tq), 1)
    acc_scr[...] = jnp.zeros_like(acc_scr)

    def scores(j):
        k = k_ref[0, pl.ds(pl.multiple_of(j * tq, tq), tq), :]
        return lax.dot_general(qs, k, NT, preferred_element_type=F32)

    def update(j, s, m, l, masked):
        v = v_ref[0, pl.ds(pl.multiple_of(j * tq, tq), tq), :]
        s = s + slope * ((j - qi) * tq + kio).astype(F32)
        if masked:
            row = lax.broadcasted_iota(jnp.int32, (2 * tq, tq), 0)
            col = lax.broadcasted_iota(jnp.int32, (2 * tq, tq), 1)
            s = jnp.where(col <= jnp.where(row >= tq, row - tq, row), s, -1e30)
        m_new = jnp.maximum(m, jnp.max(s, axis=-1, keepdims=True))
        alpha = jnp.exp(m - m_new)
        p = jnp.exp(s - m_new)
        l_new = alpha * l + jnp.sum(p, axis=-1, keepdims=True)
        acc_scr[...] = alpha * acc_scr[...] + jnp.dot(p.astype(BF16), v, preferred_element_type=F32)
        return m_new, l_new

    def body(j, carry):
        m, l, s = carry
        s_next = scores(j + 1)
        m, l = update(j, s, m, l, False)
        return m, l, s_next

    m0 = jnp.full((2 * tq, 1), -1e30, F32)
    l0 = jnp.zeros((2 * tq, 1), F32)
    m, l, s = lax.fori_loop(0, qi, body, (m0, l0, scores(0)))
    m, l = update(qi, s, m, l, True)

    o = acc_scr[...] * (1.0 / l)
    lqk = lqk_ref[...]
    lam = (jnp.exp(jnp.sum(lqk[0:1] * lqk[1:2], axis=-1, keepdims=True))
           - jnp.exp(jnp.sum(lqk[2:3] * lqk[3:4], axis=-1, keepdims=True)) + lam_init)
    o = o[:tq] - lam * o[tq:]
    o = o * lax.rsqrt(jnp.mean(o * o, axis=-1, keepdims=True) + EPS) * subln_ref[...] * (1.0 - lam_init)
    z = z_ref[0].astype(F32)
    o_ref[0] = (o * (z * _sigmoid(z))).astype(BF16)


def _attn(pm3, lqk, subln, *, tq, lam_init):
    b, s, _ = pm3.shape
    nq = ATT_W // LANES
    return pl.pallas_call(
        functools.partial(_attn_body, tq=tq, lam_init=lam_init),
        grid=(b, ATT_HEADS, s // tq),
        in_specs=[
            pl.BlockSpec((1, tq, LANES), lambda bi, h, qi: (bi, qi, h)),
            pl.BlockSpec((1, s, LANES), lambda bi, h, qi: (bi, 0, nq + h)),
            pl.BlockSpec((1, s, LANES), lambda bi, h, qi: (bi, 0, 2 * nq + h)),
            pl.BlockSpec((1, tq, LANES), lambda bi, h, qi: (bi, qi, 3 * nq + h)),
            pl.BlockSpec((4, ATT_HD), lambda bi, h, qi: (0, 0)),
            pl.BlockSpec((1, ATT_VD), lambda bi, h, qi: (0, 0)),
        ],
        out_specs=pl.BlockSpec((1, tq, LANES), lambda bi, h, qi: (bi, qi, h)),
        out_shape=jax.ShapeDtypeStruct((b, s, ATT_W), BF16),
        scratch_shapes=[pltpu.VMEM((2 * tq, ATT_VD), F32)],
        compiler_params=pltpu.CompilerParams(
            dimension_semantics=("parallel", "parallel", "arbitrary"), vmem_limit_bytes=VMEM_LIMIT),
        name="attn",
    )(pm3, pm3, pm3, pm3, lqk, subln)


DN_GROUP = 2


def _dn_body(dq_ref, dk_ref, dv_ref, dz_ref, ps_ref, rows_ref, cw_ref, acol_ref, dtcol_ref,
             arow_ref, dtrow_ref, norm_ref, ltri_ref, utri_ref, o_ref,
             st_scr, u_scr, wq_scr, kd_scr, a_scr, eg_scr, *, n_chunks):
    c_ = CHUNK
    gr = DN_GROUP * c_
    ii = lax.broadcasted_iota(jnp.int32, (c_, c_), 0)
    jj = lax.broadcasted_iota(jnp.int32, (c_, c_), 1)
    tri_incl = ii >= jj
    tri_strict = ii > jj
    eye = ii == jj
    cw = cw_ref[...]
    pad = 16
    heads = range(DN_HEADS)
    lanes = [slice(h * DN_HD, (h + 1) * DN_HD) for h in heads]

    def conv_silu(win, w4):
        y = (win[pad - 3:pad - 3 + gr] * w4[0:1] + win[pad - 2:pad - 2 + gr] * w4[1:2]
             + win[pad - 1:pad - 1 + gr] * w4[2:3] + win[pad:pad + gr] * w4[3:4])
        return y * _sigmoid(y)

    def window(ref, r0, first):
        if first:
            return jnp.concatenate([jnp.zeros((pad, DN_W), F32), ref[0, 0:gr, :].astype(F32)], axis=0)
        return ref[0, pl.ds(pl.multiple_of(r0 - pad, pad), gr + pad), :].astype(F32)

    def mm(a, b):
        return jnp.dot(a.astype(BF16), b.astype(BF16), preferred_element_type=F32)

    def prep(gi, first):
        r0 = 0 if first else pl.multiple_of(gi * gr, gr)
        cq = conv_silu(window(dq_ref, r0, first), cw[:, 0:DN_W])
        ck = conv_silu(window(dk_ref, r0, first), cw[:, DN_W:2 * DN_W])
        cv = conv_silu(window(dv_ref, r0, first), cw[:, 2 * DN_W:3 * DN_W])
        ps = ps_ref[pl.ds(r0, gr), :]
        beta_all = _sigmoid(ps)
        g_all = -jnp.exp(acol_ref[...]) * _softplus(ps + dtcol_ref[...])
        g_hi, g_lo = _split_bf16(g_all)
        chains = []
        for j in range(DN_GROUP):
            cidx = gi * DN_GROUP + j
            rs = slice(j * c_, (j + 1) * c_)
            gc_all = jnp.dot(ltri_ref[...], jnp.concatenate([g_hi[rs], g_lo[rs]], axis=0),
                             preferred_element_type=F32)
            g_row = -jnp.exp(arow_ref[...]) * _softplus(rows_ref[0, cidx] + dtrow_ref[...])
            r_hi, r_lo = _split_bf16(g_row)
            gc_rows = jnp.dot(jnp.concatenate([r_hi, r_lo], axis=1), utri_ref[...],
                              preferred_element_type=F32)
            for h in heads:
                qh, kh, vh = cq[rs, lanes[h]], ck[rs, lanes[h]], cv[rs, lanes[h]]
                qn = qh * (lax.rsqrt(jnp.sum(qh * qh, axis=-1, keepdims=True) + EPS) * DN_HD ** -0.5)
                kn = kh * lax.rsqrt(jnp.sum(kh * kh, axis=-1, keepdims=True) + EPS)
                beta = beta_all[rs, h:h + 1]
                gcc = gc_all[:, 4 + h:5 + h]
                gcr = gc_rows[4 + h:5 + h, :]
                gl = gcr[:, c_ - 1:c_]
                chains.append(dict(
                    cidx=cidx, h=h, rows=pl.ds(r0 + j * c_, c_), qn=qn, kn=kn, kb=kn * beta, vb=vh * beta,
                    gcc=gcc, gl=gl, decay=jnp.exp(jnp.where(tri_incl, gcc - gcr, -1e30))))
        for ch in chains:
            ch["gram"] = lax.dot_general(jnp.concatenate([ch["kb"], ch["qn"]], axis=0).astype(BF16),
                                         ch["kn"].astype(BF16), NT, preferred_element_type=F32)
        for ch in chains:
            x = jnp.where(tri_strict, -(ch["gram"][:c_] * ch["decay"]), 0.0)
            a_scr[ch["cidx"], ch["h"]] = (ch["gram"][c_:] * ch["decay"]).astype(BF16)
            ch["acc"] = jnp.where(eye, 1.0, x)
            ch["x"] = x
        for ch in chains:
            ch["xp"] = mm(ch["x"], ch["x"])
        for _ in range(4):
            for ch in chains:
                ch["both"] = mm(jnp.concatenate([ch["acc"], ch["xp"]], axis=0), ch["xp"])
            for ch in chains:
                ch["acc"] = ch["acc"] + ch["both"][:c_]
                ch["xp"] = ch["both"][c_:]
        for ch in chains:
            ch["last"] = mm(ch["acc"], ch["xp"])
        for ch in chains:
            t_inv = ch["acc"] + ch["last"]
            gexp = jnp.exp(ch["gcc"])
            ch["gexp"] = gexp
            ch["uw"] = mm(t_inv, jnp.concatenate([ch["vb"], ch["kb"] * gexp], axis=1))
        for ch in chains:
            cidx, h, rows = ch["cidx"], ch["h"], ch["rows"]
            u_scr[rows, lanes[h]] = ch["uw"][:, :DN_HD]
            wq_scr[cidx, 0:c_, lanes[h]] = ch["uw"][:, DN_HD:].astype(BF16)
            wq_scr[cidx, c_:2 * c_, lanes[h]] = (ch["qn"] * ch["gexp"]).astype(BF16)
            kd_scr[rows, lanes[h]] = (ch["kn"] * jnp.exp(ch["gl"] - ch["gcc"])).astype(BF16)
            eg_scr[cidx, h] = jnp.broadcast_to(jnp.exp(ch["gl"]), (8, DN_HD))

    prep(0, True)

    def prep_body(gi, carry):
        prep(gi, False)
        return carry

    lax.fori_loop(1, n_chunks // DN_GROUP, prep_body, 0)

    st_scr[...] = jnp.zeros_like(st_scr)

    def scan(c, carry):
        rows = pl.ds(pl.multiple_of(c * c_, c_), c_)
        states = [st_scr[h] for h in heads]
        wq = [jnp.dot(wq_scr[c, :, lanes[h]], states[h].astype(BF16), preferred_element_type=F32)
              for h in heads]
        v_new = [(u_scr[rows, lanes[h]] - wq[h][:c_]).astype(BF16) for h in heads]
        o = [wq[h][c_:] + jnp.dot(a_scr[c, h], v_new[h], preferred_element_type=F32) for h in heads]
        for h in heads:
            st_scr[h] = states[h] * eg_scr[c, h][0:1, :] + lax.dot_general(
                kd_scr[rows, lanes[h]], v_new[h], TN, preferred_element_type=F32)
        for h in heads:
            on = o[h] * lax.rsqrt(jnp.mean(o[h] * o[h], axis=-1, keepdims=True) + EPS) * norm_ref[...]
            z = dz_ref[0, rows, lanes[h]].astype(F32)
            o_ref[0, rows, lanes[h]] = (on * (z * _sigmoid(z))).astype(BF16)
        return carry

    lax.fori_loop(0, n_chunks, scan, 0)


def _deltanet(pm3, ps, rows4, cw, acol, dtcol, arow, dtrow, norm, ltri, utri):
    b, s, _ = pm3.shape
    n_chunks = s // CHUNK
    assert n_chunks % DN_GROUP == 0
    base = 4 * ATT_W // DN_W
    full = lambda shape: pl.BlockSpec(shape, lambda bi: (0,) * len(shape))
    return pl.pallas_call(
        functools.partial(_dn_body, n_chunks=n_chunks),
        grid=(b,),
        in_specs=[
            pl.BlockSpec((1, s, DN_W), lambda bi: (bi, 0, base)),
            pl.BlockSpec((1, s, DN_W), lambda bi: (bi, 0, base + 1)),
            pl.BlockSpec((1, s, DN_W), lambda bi: (bi, 0, base + 2)),
            pl.BlockSpec((1, s, DN_W), lambda bi: (bi, 0, base + 3)),
            pl.BlockSpec((s, D_SMALL), lambda bi: (bi, 0)),
            pl.BlockSpec((1, n_chunks, 8, CHUNK), lambda bi: (bi, 0, 0, 0)),
            full((DN_CONV, 3 * DN_W)),
            full((1, D_SMALL)), full((1, D_SMALL)), full((8, 1)), full((8, 1)),
            full((1, DN_HD)),
            full((CHUNK, 2 * CHUNK)), full((2 * CHUNK, CHUNK)),
        ],
        out_specs=pl.BlockSpec((1, s, DN_W), lambda bi: (bi, 0, 0)),
        out_shape=jax.ShapeDtypeStruct((b, s, DN_W), BF16),
        scratch_shapes=[
            pltpu.VMEM((DN_HEADS, DN_HD, DN_HD), F32),
            pltpu.VMEM((s, DN_W), F32),
            pltpu.VMEM((n_chunks, 2 * CHUNK, DN_W), BF16),
            pltpu.VMEM((s, DN_W), BF16),
            pltpu.VMEM((n_chunks, DN_HEADS, CHUNK, CHUNK), BF16),
            pltpu.VMEM((n_chunks, DN_HEADS, 8, DN_HD), F32),
        ],
        compiler_params=pltpu.CompilerParams(
            dimension_semantics=("parallel",), vmem_limit_bytes=VMEM_LIMIT),
        name="deltanet",
    )(pm3, pm3, pm3, pm3, ps, rows4, cw, acol, dtcol, arow, dtrow, norm, ltri, utri)


def _gla_body(gq_ref, gk_ref, gv_ref, gz_ref, ps_ref, w2_ref, b2_ref, norm_ref, cmat_ref,
              o_ref, st_scr, *, n_chunks):
    c_ = CHUNK
    st_scr[...] = jnp.zeros_like(st_scr)
    rowi = lax.broadcasted_iota(jnp.int32, (c_, 1), 0)
    upper = [(rowi % (2 * s)) >= s for s in GLA_LEVELS]
    ii = lax.broadcasted_iota(jnp.int32, (2 * c_, c_), 0) % c_
    jj = lax.broadcasted_iota(jnp.int32, (2 * c_, c_), 1)
    same = [(ii // (2 * s)) == (jj // (2 * s)) for s in GLA_LEVELS]
    eye = ii == jj
    lane = lax.broadcasted_iota(jnp.int32, (c_, LANES), 1)
    first_head = lane < GLA_KD
    lane_sq = lax.broadcasted_iota(jnp.int32, (GLA_VD, LANES), 1)
    head_lanes = [lane_sq < GLA_KD, lane_sq >= GLA_KD]

    def stack_heads(t):
        return jnp.concatenate([jnp.where(first_head, t, 0.0), jnp.where(first_head, 0.0, t)], axis=0)

    def body(c, carry):
        r0 = pl.multiple_of(c * c_, c_)
        rows = pl.ds(r0, c_)
        pre = jnp.dot(ps_ref[rows, :].astype(BF16), w2_ref[...], preferred_element_type=F32) + b2_ref[...]
        gk = (jnp.minimum(pre, 0.0) - jnp.log(1.0 + jnp.exp(-jnp.abs(pre)))) * (1.0 / GLA_TAU)
        g_hi, g_lo = _split_bf16(gk)
        br = jnp.dot(cmat_ref[...], jnp.concatenate([g_hi, g_lo], axis=0), preferred_element_type=F32)
        for p in range(GLA_HEADS // 2):
            pls = slice(p * LANES, (p + 1) * LANES)
            bcum = br[0:c_, pls]
            q = gq_ref[0, rows, pls].astype(F32) * (GLA_KD ** -0.5)
            k = gk_ref[0, rows, pls].astype(F32)
            kb16 = k.astype(BF16)
            att = jnp.where(eye, lax.dot_general(stack_heads(q).astype(BF16), kb16, NT,
                                                 preferred_element_type=F32), 0.0)
            for li in range(len(GLA_LEVELS)):
                ref = br[c_ * (li + 1):c_ * (li + 2), pls]
                up = upper[li]
                e = jnp.exp(jnp.where(up, bcum - ref, ref - bcum))
                qt = jnp.where(up, q * e, 0.0)
                kt = jnp.where(up, 0.0, k * e)
                part = lax.dot_general(stack_heads(qt).astype(BF16), kt.astype(BF16), NT,
                                       preferred_element_type=F32)
                att = att + jnp.where(same[li], part, 0.0)
            blast = bcum[c_ - 1:c_, :]
            qe = (q * jnp.exp(bcum)).astype(BF16)
            ke = (k * jnp.exp(blast - bcum)).astype(BF16)
            dec = jnp.exp(blast)
            for hh in range(2):
                h = 2 * p + hh
                hs = slice(h * GLA_VD, (h + 1) * GLA_VD)
                vh = gv_ref[0, rows, hs]
                state = st_scr[h]
                o = (lax.dot_general(qe, state.astype(BF16), NT, preferred_element_type=F32)
                     + jnp.dot(att[hh * c_:(hh + 1) * c_].astype(BF16), vh, preferred_element_type=F32))
                upd = lax.dot_general(vh, ke, TN, preferred_element_type=F32)
                st_scr[h] = state * dec + jnp.where(head_lanes[hh], upd, 0.0)
                o = o * lax.rsqrt(jnp.mean(o * o, axis=-1, keepdims=True) + EPS) * norm_ref[...]
                z = gz_ref[0, rows, hs].astype(F32)
                o_ref[0, rows, hs] = (o * (z * _sigmoid(z))).astype(BF16)
        return carry

    lax.fori_loop(0, n_chunks, body, 0)


def _gla(pm3, ps, w2, b2, norm, cmat):
    b, s, _ = pm3.shape
    n_chunks = s // CHUNK
    off = 4 * ATT_W + 4 * DN_W
    full = lambda shape: pl.BlockSpec(shape, lambda bi: (0,) * len(shape))
    return pl.pallas_call(
        functools.partial(_gla_body, n_chunks=n_chunks),
        grid=(b,),
        in_specs=[
            pl.BlockSpec((1, s, GLA_KW), lambda bi: (bi, 0, off // GLA_KW)),
            pl.BlockSpec((1, s, GLA_KW), lambda bi: (bi, 0, off // GLA_KW + 1)),
            pl.BlockSpec((1, s, GLA_W), lambda bi: (bi, 0, (off + 2 * GLA_KW) // GLA_W)),
            pl.BlockSpec((1, s, GLA_W), lambda bi: (bi, 0, (off + 2 * GLA_KW) // GLA_W + 1)),
            pl.BlockSpec((s, D_SMALL), lambda bi: (bi, 0)),
            full((D_SMALL, GLA_KW)), full((1, GLA_KW)), full((1, GLA_VD)),
            full(((len(GLA_LEVELS) + 1) * CHUNK, 2 * CHUNK)),
        ],
        out_specs=pl.BlockSpec((1, s, GLA_W), lambda bi: (bi, 0, 0)),
        out_shape=jax.ShapeDtypeStruct((b, s, GLA_W), BF16),
        scratch_shapes=[pltpu.VMEM((GLA_HEADS, GLA_VD, LANES), F32)],
        compiler_params=pltpu.CompilerParams(
            dimension_semantics=("parallel",), vmem_limit_bytes=VMEM_LIMIT),
        name="gla",
    )(pm3, pm3, pm3, pm3, ps, w2, b2, norm, cmat)


def _out_body(x_ref, ya_ref, yd_ref, yg_ref, p_ref, wo_ref, postg_ref, pg_ref, pp_ref, pn_ref, o_ref):
    y = (jnp.dot(ya_ref[...], wo_ref[0:ATT_W, :], preferred_element_type=F32)
         + jnp.dot(yd_ref[...], wo_ref[ATT_W:ATT_W + DN_W, :], preferred_element_type=F32)
         + jnp.dot(yg_ref[...], wo_ref[ATT_W + DN_W:D_MIX, :], preferred_element_type=F32))
    x1 = x_ref[...] + y * lax.rsqrt(jnp.mean(y * y, axis=-1, keepdims=True) + EPS) * postg_ref[...]
    gate = _sigmoid(jnp.dot(x1.astype(BF16), pg_ref[...], preferred_element_type=F32))
    pe = jnp.dot(p_ref[...].astype(BF16), pp_ref[...], preferred_element_type=F32) * gate
    o_ref[...] = x1 + pe * lax.rsqrt(jnp.mean(pe * pe, axis=-1, keepdims=True) + EPS) * pn_ref[...]


def _out(x2, ya, yd, yg, p2, wo, postg, pg, pp, pn, *, tm):
    t = x2.shape[0]
    tile = lambda w: pl.BlockSpec((tm, w), lambda i: (i, 0))
    full = lambda shape: pl.BlockSpec(shape, lambda i: (0, 0))
    return pl.pallas_call(
        _out_body,
        grid=(t // tm,),
        in_specs=[tile(D_MODEL), tile(ATT_W), tile(DN_W), tile(GLA_W), tile(D_PLE),
                  full((D_MIX, D_MODEL)), full((1, D_MODEL)), full((D_MODEL, D_MODEL)),
                  full((D_PLE, D_MODEL)), full((1, D_MODEL))],
        out_specs=tile(D_MODEL),
        out_shape=jax.ShapeDtypeStruct((t, D_MODEL), F32),
        compiler_params=pltpu.CompilerParams(
            dimension_semantics=("parallel",), vmem_limit_bytes=VMEM_LIMIT),
        name="out",
    )(x2, ya, yd, yg, p2, wo, postg, pg, pp, pn)


def _chunk_constants():
    c = CHUNK
    ltri = np.tril(np.ones((c, c), np.float32))
    blocks = [ltri]
    idx = np.arange(c)
    for s in GLA_LEVELS:
        bnd = (idx // (2 * s)) * (2 * s) + s - 1
        blocks.append(ltri[bnd])
    cmat = np.concatenate(blocks, axis=0)
    return (jnp.asarray(np.concatenate([ltri, ltri], axis=1), BF16),
            jnp.asarray(np.concatenate([ltri.T, ltri.T], axis=0), BF16),
            jnp.asarray(np.concatenate([cmat, cmat], axis=1), BF16))


def _pick(n, pref):
    for t in pref:
        if n % t == 0:
            return t
    return n


def _layer(x, p_l, w_in, w_out, pre_g, post_g, lq1, lk1, lq2, lk2, att_subln, dn_conv, dn_a_log,
           dn_dt_bias, dn_norm, gla_w2, gla_b, gla_norm, ple_proj, ple_gate, ple_norm, layer_idx, consts):
    b, s, _ = x.shape
    t = b * s
    ltri, utri, cmat = consts
    lam_init = 0.8 - 0.6 * math.exp(-0.3 * layer_idx)

    n_big = 4 * ATT_W + 4 * DN_W
    n_small = 2 * DN_HEADS
    wm = jnp.concatenate([w_in[:, :n_big], w_in[:, n_big + n_small:n_big + n_small + 2 * GLA_KW + 2 * GLA_W]],
                         axis=1).astype(BF16)
    w_small = jnp.concatenate([w_in[:, n_big:n_big + n_small], w_in[:, -GLA_RANK:]], axis=1)
    ws = jnp.pad(w_small, ((0, 0), (0, D_SMALL - w_small.shape[1]))).astype(BF16)
    wst = w_small[:, :n_small].T.astype(BF16)

    x2 = x.reshape(t, D_MODEL)
    pm, ps, pst = _proj(x2, pre_g.reshape(1, D_MODEL), wm, ws, wst,
                        tm=_pick(t, (1024, 512, 256)), tn=_pick(D_MAIN, (1536, 1280, 768)))
    pm3 = pm.reshape(b, s, D_MAIN)

    lqk = jnp.stack([lq1, lk1, lq2, lk2], axis=0)
    y_att = _attn(pm3, lqk, att_subln.reshape(1, ATT_VD), tq=_pick(s, (256, 128)), lam_init=lam_init)

    rows4 = pst.reshape(8, b, s // CHUNK, CHUNK).transpose(1, 2, 0, 3)
    lane_vec = lambda v, off: jnp.zeros((1, D_SMALL), F32).at[0, off:off + v.shape[0]].set(v)
    row_vec = lambda v, off: jnp.zeros((8, 1), F32).at[off:off + v.shape[0], 0].set(v)
    y_dn = _deltanet(pm3, ps, rows4, dn_conv, lane_vec(dn_a_log, DN_HEADS), lane_vec(dn_dt_bias, DN_HEADS),
                     row_vec(dn_a_log, DN_HEADS), row_vec(dn_dt_bias, DN_HEADS),
                     dn_norm.reshape(1, DN_HD), ltri, utri)

    w2 = jnp.zeros((D_SMALL, GLA_KW), F32).at[n_small:n_small + GLA_RANK].set(gla_w2).astype(BF16)
    y_gla = _gla(pm3, ps, w2, gla_b.reshape(1, GLA_KW), gla_norm.reshape(1, GLA_VD), cmat)

    out = _out(x2, y_att.reshape(t, ATT_W), y_dn.reshape(t, DN_W), y_gla.reshape(t, GLA_W),
               p_l.reshape(t, D_PLE), w_out.astype(BF16), post_g.reshape(1, D_MODEL),
               ple_gate.astype(BF16), ple_proj.astype(BF16), ple_norm.reshape(1, D_MODEL),
               tm=_pick(t, (512, 256)))
    return out.reshape(b, s, D_MODEL)


def kernel(x, p, w_in, w_out, pre_gain, post_gain, att_lq1, att_lk1, att_lq2, att_lk2, att_subln, dn_conv, dn_a_log, dn_dt_bias, dn_norm, gla_w2, gla_b, gla_norm, ple_proj, ple_gate, ple_norm):
    consts = _chunk_constants()
    for i in range(p.shape[0]):
        x = _layer(x, p[i], w_in[i], w_out[i], pre_gain[i], post_gain[i], att_lq1[i], att_lk1[i],
                   att_lq2[i], att_lk2[i], att_subln[i], dn_conv[i], dn_a_log[i], dn_dt_bias[i],
                   dn_norm[i], gla_w2[i], gla_b[i], gla_norm[i], ple_proj[i], ple_gate[i],
                   ple_norm[i], i, consts)
    return x
```

```python
import functools
import math

import numpy as np
import jax
import jax.numpy as jnp
from jax import lax
from jax.experimental import pallas as pl
from jax.experimental.pallas import tpu as pltpu

F32 = jnp.float32
BF16 = jnp.bfloat16

D_MODEL = 1024
D_PLE = 256
EPS = 1e-6
ATT_HEADS = 8
ATT_HD = 64
ATT_VD = 128
ATT_W = 1024
DN_HEADS = 4
DN_HD = 128
DN_W = 512
DN_CONV = 4
GLA_HEADS = 4
GLA_KD = 64
GLA_VD = 128
GLA_KW = 256
GLA_W = 512
GLA_RANK = 16
GLA_TAU = 16.0
CHUNK = 64
D_MIX = 2048
D_MAIN = 6656
D_SMALL = 128
LANES = 128
VMEM_LIMIT = 48 * 1024 * 1024

NT = (((1,), (1,)), ((), ()))
TN = (((0,), (0,)), ((), ()))
GLA_LEVELS = (32, 16, 8, 4, 2, 1)


def _sigmoid(x):
    return 1.0 / (1.0 + jnp.exp(-x))


def _softplus(x):
    return jnp.maximum(x, 0.0) + jnp.log(1.0 + jnp.exp(-jnp.abs(x)))


def _split_bf16(x):
    hi = x.astype(BF16)
    lo = (x - hi.astype(F32)).astype(BF16)
    return hi, lo


def _proj_body(x_ref, g_ref, wm_ref, ws_ref, wst_ref, wvt_ref, pm_ref, ps_ref, pst_ref, vt_ref, h_scr):
    @pl.when(pl.program_id(1) == 0)
    def _():
        x = x_ref[...]
        h = (x * lax.rsqrt(jnp.mean(x * x, axis=-1, keepdims=True) + EPS) * g_ref[...]).astype(BF16)
        h_scr[...] = h
        ps_ref[...] = jnp.dot(h, ws_ref[...], preferred_element_type=F32)
        pst_ref[...] = lax.dot_general(wst_ref[...], h, NT, preferred_element_type=F32)
        vt_ref[0] = lax.dot_general(wvt_ref[...], h, NT, preferred_element_type=F32).astype(BF16)

    pm_ref[...] = jnp.dot(h_scr[...], wm_ref[...], preferred_element_type=F32).astype(BF16)


def _proj(x2, g, wm, ws, wst, wvt, *, b, s, tm, tn):
    t = x2.shape[0]
    spb = s // tm
    return pl.pallas_call(
        _proj_body,
        grid=(t // tm, D_MAIN // tn),
        in_specs=[
            pl.BlockSpec((tm, D_MODEL), lambda i, j: (i, 0)),
            pl.BlockSpec((1, D_MODEL), lambda i, j: (0, 0)),
            pl.BlockSpec((D_MODEL, tn), lambda i, j: (0, j)),
            pl.BlockSpec((D_MODEL, D_SMALL), lambda i, j: (0, 0)),
            pl.BlockSpec((8, D_MODEL), lambda i, j: (0, 0)),
            pl.BlockSpec((ATT_W, D_MODEL), lambda i, j: (0, 0)),
        ],
        out_specs=[
            pl.BlockSpec((tm, tn), lambda i, j: (i, j)),
            pl.BlockSpec((tm, D_SMALL), lambda i, j: (i, 0)),
            pl.BlockSpec((8, tm), lambda i, j: (0, i)),
            pl.BlockSpec((1, ATT_W, tm), lambda i, j: (i // spb, 0, i % spb)),
        ],
        out_shape=[
            jax.ShapeDtypeStruct((t, D_MAIN), BF16),
            jax.ShapeDtypeStruct((t, D_SMALL), F32),
            jax.ShapeDtypeStruct((8, t), F32),
            jax.ShapeDtypeStruct((b, ATT_W, s), BF16),
        ],
        scratch_shapes=[pltpu.VMEM((tm, D_MODEL), BF16)],
        compiler_params=pltpu.CompilerParams(
            dimension_semantics=("parallel", "arbitrary"), vmem_limit_bytes=VMEM_LIMIT),
        name="proj",
    )(x2, g, wm, ws, wst, wvt)


def _attn_body(q_ref, k_ref, vt_ref, z_ref, lqk_ref, subln_ref, o_ref, acc_scr, *, tq, tk, n_q, lam_init):
    h = pl.program_id(1)
    sub = LANES
    nsub = tk // sub
    nch = tk // tq
    nr = 2 * tq
    lane = lax.broadcasted_iota(jnp.int32, (tq, LANES), 1)
    scale = jnp.asarray(ATT_HD ** -0.5, BF16)
    slope_row = jnp.exp2(jnp.full((1, nr), -8.0 / ATT_HEADS, F32) * (h + 1).astype(F32))
    krow = lax.broadcasted_iota(jnp.int32, (sub, nr), 0)
    qcol = lax.broadcasted_iota(jnp.int32, (sub, nr), 1)
    lag = krow - jnp.where(qcol >= tq, qcol - tq, qcol)
    bias = slope_row * krow.astype(F32)
    lqk = lqk_ref[...]
    lam = (jnp.exp(jnp.sum(lqk[0:1] * lqk[1:2], axis=-1, keepdims=True))
           - jnp.exp(jnp.sum(lqk[2:3] * lqk[3:4], axis=-1, keepdims=True)) + lam_init)

    def q_block(qb, carry):
        qbase = pl.multiple_of(qb * tk, tk)
        qs = []
        for c in range(nch):
            q = q_ref[0, pl.ds(qbase + c * tq, tq), :]
            zero = jnp.zeros_like(q)
            qs.append(jnp.concatenate([jnp.where(lane < ATT_HD, q, zero),
                                       jnp.where(lane >= ATT_HD, q, zero)], axis=0) * scale)
        acc_scr[...] = jnp.zeros_like(acc_scr)

        def update(j, ms, ls, diag):
            koff = pl.multiple_of(j * tk, tk)
            k = k_ref[0, pl.ds(koff, tk), :]
            vt = vt_ref[0, :, pl.ds(koff, tk)]
            chains = range(nch)
            ns = [c + 1 if diag else nsub for c in chains]
            s_all = [lax.dot_general(k[:ns[c] * sub], qs[c], NT, preferred_element_type=F32) for c in chains]
            s, shift, ms_new = [], [], []
            for c in chains:
                sc, shc = [], []
                m_new = ms[c]
                for t in range(ns[c]):
                    st = s_all[c][t * sub:(t + 1) * sub] + bias
                    if diag and t == c:
                        st = jnp.where(lag <= 0, st, -1e30)
                    sh = slope_row * (koff - qbase + (t - c) * sub).astype(F32)
                    m_new = jnp.maximum(m_new, jnp.max(st, axis=0, keepdims=True) + sh)
                    sc.append(st)
                    shc.append(sh)
                s.append(sc)
                shift.append(shc)
                ms_new.append(m_new)
            alpha, ls_new, p = [], [], []
            for c in chains:
                a = jnp.exp(ms[c] - ms_new[c])
                l_new = a * ls[c]
                pc = []
                for t in range(ns[c]):
                    pt = jnp.exp(s[c][t] - (ms_new[c] - shift[c][t]))
                    l_new = l_new + jnp.sum(pt, axis=0, keepdims=True)
                    pc.append(pt.astype(BF16))
                alpha.append(a)
                ls_new.append(l_new)
                p.append(jnp.concatenate(pc, axis=0))
            pv = [jnp.dot(vt[:, :ns[c] * sub], p[c], preferred_element_type=F32) for c in chains]
            for c in chains:
                acc_scr[c] = alpha[c] * acc_scr[c] + pv[c]
            return tuple(ms_new), tuple(ls_new)

        m0 = tuple(jnp.full((1, nr), -1e30, F32) for _ in range(nch))
        l0 = tuple(jnp.zeros((1, nr), F32) for _ in range(nch))
        ms, ls = lax.fori_loop(0, qb, lambda j, c: update(j, c[0], c[1], False), (m0, l0))
        ms, ls = update(qb, ms, ls, True)

        for c in range(nch):
            rows = pl.ds(qbase + c * tq, tq)
            ot = acc_scr[c] * (1.0 / ls[c])
            o = (ot[:, :tq] - lam * ot[:, tq:]).T
            o = o * lax.rsqrt(jnp.mean(o * o, axis=-1, keepdims=True) + EPS) * subln_ref[...] * (1.0 - lam_init)
            z = z_ref[0, rows, :].astype(F32)
            o_ref[0, rows, :] = (o * (z * _sigmoid(z))).astype(BF16)
        return carry

    lax.fori_loop(0, n_q // nch, q_block, 0)


def _attn(pm3, vt, lqk, subln, *, tq, lam_init):
    b, s, _ = pm3.shape
    nq = ATT_W // LANES
    seq = lambda col: pl.BlockSpec((1, s, LANES), lambda bi, h: (bi, 0, col + h))
    tk = _pick(s, (512, 256, 128))
    return pl.pallas_call(
        functools.partial(_attn_body, tq=tq, tk=tk, n_q=s // tq, lam_init=lam_init),
        grid=(b, ATT_HEADS),
        in_specs=[
            seq(0), seq(nq),
            pl.BlockSpec((1, LANES, s), lambda bi, h: (bi, h, 0)),
            seq(2 * nq),
            pl.BlockSpec((4, ATT_HD), lambda bi, h: (0, 0)),
            pl.BlockSpec((1, ATT_VD), lambda bi, h: (0, 0)),
        ],
        out_specs=pl.BlockSpec((1, s, LANES), lambda bi, h: (bi, 0, h)),
        out_shape=jax.ShapeDtypeStruct((b, s, ATT_W), BF16),
        scratch_shapes=[pltpu.VMEM((tk // tq, ATT_VD, 2 * tq), F32)],
        compiler_params=pltpu.CompilerParams(
            dimension_semantics=("parallel", "parallel"), vmem_limit_bytes=VMEM_LIMIT),
        name="attn",
    )(pm3, pm3, vt, pm3, lqk, subln)


DN_GROUP = 2


def _dn_body(dq_ref, dk_ref, dv_ref, dz_ref, ps_ref, rows_ref, cw_ref, acol_ref, dtcol_ref,
             arow_ref, dtrow_ref, norm_ref, ltri_ref, utri_ref, o_ref,
             st_scr, u_scr, wq_scr, kd_scr, a_scr, eg_scr, *, n_chunks):
    c_ = CHUNK
    gr = DN_GROUP * c_
    ii = lax.broadcasted_iota(jnp.int32, (c_, c_), 0)
    jj = lax.broadcasted_iota(jnp.int32, (c_, c_), 1)
    tri_incl = ii >= jj
    tri_strict = ii > jj
    eye = ii == jj
    cw = cw_ref[...]
    pad = 16
    heads = range(DN_HEADS)
    lanes = [slice(h * DN_HD, (h + 1) * DN_HD) for h in heads]

    def conv_silu(win, w4):
        y = (win[pad - 3:pad - 3 + gr] * w4[0:1] + win[pad - 2:pad - 2 + gr] * w4[1:2]
             + win[pad - 1:pad - 1 + gr] * w4[2:3] + win[pad:pad + gr] * w4[3:4])
        return y * _sigmoid(y)

    def window(ref, r0, first):
        if first:
            return jnp.concatenate([jnp.zeros((pad, DN_W), F32), ref[0, 0:gr, :].astype(F32)], axis=0)
        return ref[0, pl.ds(pl.multiple_of(r0 - pad, pad), gr + pad), :].astype(F32)

    def mm(a, b):
        return jnp.dot(a.astype(BF16), b.astype(BF16), preferred_element_type=F32)

    def prep(gi, first):
        r0 = 0 if first else pl.multiple_of(gi * gr, gr)
        cq = conv_silu(window(dq_ref, r0, first), cw[:, 0:DN_W])
        ck = conv_silu(window(dk_ref, r0, first), cw[:, DN_W:2 * DN_W])
        cv = conv_silu(window(dv_ref, r0, first), cw[:, 2 * DN_W:3 * DN_W])
        ps = ps_ref[pl.ds(r0, gr), :]
        beta_all = _sigmoid(ps)
        g_all = -jnp.exp(acol_ref[...]) * _softplus(ps + dtcol_ref[...])
        g_hi, g_lo = _split_bf16(g_all)
        chains = []
        for j in range(DN_GROUP):
            cidx = gi * DN_GROUP + j
            rs = slice(j * c_, (j + 1) * c_)
            gc_all = jnp.dot(ltri_ref[...], jnp.concatenate([g_hi[rs], g_lo[rs]], axis=0),
                             preferred_element_type=F32)
            g_row = -jnp.exp(arow_ref[...]) * _softplus(rows_ref[0, cidx] + dtrow_ref[...])
            r_hi, r_lo = _split_bf16(g_row)
            gc_rows = jnp.dot(jnp.concatenate([r_hi, r_lo], axis=1), utri_ref[...],
                              preferred_element_type=F32)
            for h in heads:
                qh, kh, vh = cq[rs, lanes[h]], ck[rs, lanes[h]], cv[rs, lanes[h]]
                qn = qh * (lax.rsqrt(jnp.sum(qh * qh, axis=-1, keepdims=True) + EPS) * DN_HD ** -0.5)
                kn = kh * lax.rsqrt(jnp.sum(kh * kh, axis=-1, keepdims=True) + EPS)
                beta = beta_all[rs, h:h + 1]
                gcc = gc_all[:, 4 + h:5 + h]
                gcr = gc_rows[4 + h:5 + h, :]
                gl = gcr[:, c_ - 1:c_]
                chains.append(dict(
                    cidx=cidx, h=h, rows=pl.ds(r0 + j * c_, c_), qn=qn, kn=kn, kb=kn * beta, vb=vh * beta,
                    gcc=gcc, gl=gl, decay=jnp.exp(jnp.where(tri_incl, gcc - gcr, -1e30))))
        for ch in chains:
            ch["gram"] = lax.dot_general(jnp.concatenate([ch["kb"], ch["qn"]], axis=0).astype(BF16),
                                         ch["kn"].astype(BF16), NT, preferred_element_type=F32)
        for ch in chains:
            x = jnp.where(tri_strict, -(ch["gram"][:c_] * ch["decay"]), 0.0)
            a_scr[ch["cidx"], ch["h"]] = (ch["gram"][c_:] * ch["decay"]).astype(BF16)
            ch["acc"] = jnp.where(eye, 1.0, x)
            ch["x"] = x
        for ch in chains:
            ch["xp"] = mm(ch["x"], ch["x"])
        for _ in range(4):
            for ch in chains:
                ch["both"] = mm(jnp.concatenate([ch["acc"], ch["xp"]], axis=0), ch["xp"])
            for ch in chains:
                ch["acc"] = ch["acc"] + ch["both"][:c_]
                ch["xp"] = ch["both"][c_:]
        for ch in chains:
            ch["last"] = mm(ch["acc"], ch["xp"])
        for ch in chains:
            t_inv = ch["acc"] + ch["last"]
            gexp = jnp.exp(ch["gcc"])
            ch["gexp"] = gexp
            ch["uw"] = mm(t_inv, jnp.concatenate([ch["vb"], ch["kb"] * gexp], axis=1))
        for ch in chains:
            cidx, h, rows = ch["cidx"], ch["h"], ch["rows"]
            u_scr[rows, lanes[h]] = ch["uw"][:, :DN_HD]
            wq_scr[cidx, 0:c_, lanes[h]] = ch["uw"][:, DN_HD:].astype(BF16)
            wq_scr[cidx, c_:2 * c_, lanes[h]] = (ch["qn"] * ch["gexp"]).astype(BF16)
            kd_scr[rows, lanes[h]] = (ch["kn"] * jnp.exp(ch["gl"] - ch["gcc"])).astype(BF16)
            eg_scr[cidx, h] = jnp.broadcast_to(jnp.exp(ch["gl"]), (8, DN_HD))

    prep(0, True)

    def prep_body(gi, carry):
        prep(gi, False)
        return carry

    lax.fori_loop(1, n_chunks // DN_GROUP, prep_body, 0)

    st_scr[...] = jnp.zeros_like(st_scr)

    def scan(c, carry):
        rows = pl.ds(pl.multiple_of(c * c_, c_), c_)
        states = [st_scr[h] for h in heads]
        wq = [jnp.dot(wq_scr[c, :, lanes[h]], states[h].astype(BF16), preferred_element_type=F32)
              for h in heads]
        v_new = [(u_scr[rows, lanes[h]] - wq[h][:c_]).astype(BF16) for h in heads]
        o = [wq[h][c_:] + jnp.dot(a_scr[c, h], v_new[h], preferred_element_type=F32) for h in heads]
        for h in heads:
            st_scr[h] = states[h] * eg_scr[c, h][0:1, :] + lax.dot_general(
                kd_scr[rows, lanes[h]], v_new[h], TN, preferred_element_type=F32)
        for h in heads:
            on = o[h] * lax.rsqrt(jnp.mean(o[h] * o[h], axis=-1, keepdims=True) + EPS) * norm_ref[...]
            z = dz_ref[0, rows, lanes[h]].astype(F32)
            o_ref[0, rows, lanes[h]] = (on * (z * _sigmoid(z))).astype(BF16)
        return carry

    lax.fori_loop(0, n_chunks, scan, 0)


def _deltanet(pm3, ps, rows4, cw, acol, dtcol, arow, dtrow, norm, ltri, utri):
    b, s, _ = pm3.shape
    n_chunks = s // CHUNK
    assert n_chunks % DN_GROUP == 0
    base = 3 * ATT_W // DN_W
    full = lambda shape: pl.BlockSpec(shape, lambda bi: (0,) * len(shape))
    return pl.pallas_call(
        functools.partial(_dn_body, n_chunks=n_chunks),
        grid=(b,),
        in_specs=[
            pl.BlockSpec((1, s, DN_W), lambda bi: (bi, 0, base)),
            pl.BlockSpec((1, s, DN_W), lambda bi: (bi, 0, base + 1)),
            pl.BlockSpec((1, s, DN_W), lambda bi: (bi, 0, base + 2)),
            pl.BlockSpec((1, s, DN_W), lambda bi: (bi, 0, base + 3)),
            pl.BlockSpec((s, D_SMALL), lambda bi: (bi, 0)),
            pl.BlockSpec((1, n_chunks, 8, CHUNK), lambda bi: (bi, 0, 0, 0)),
            full((DN_CONV, 3 * DN_W)),
            full((1, D_SMALL)), full((1, D_SMALL)), full((8, 1)), full((8, 1)),
            full((1, DN_HD)),
            full((CHUNK, 2 * CHUNK)), full((2 * CHUNK, CHUNK)),
        ],
        out_specs=pl.BlockSpec((1, s, DN_W), lambda bi: (bi, 0, 0)),
        out_shape=jax.ShapeDtypeStruct((b, s, DN_W), BF16),
        scratch_shapes=[
            pltpu.VMEM((DN_HEADS, DN_HD, DN_HD), F32),
            pltpu.VMEM((s, DN_W), F32),
            pltpu.VMEM((n_chunks, 2 * CHUNK, DN_W), BF16),
            pltpu.VMEM((s, DN_W), BF16),
            pltpu.VMEM((n_chunks, DN_HEADS, CHUNK, CHUNK), BF16),
            pltpu.VMEM((n_chunks, DN_HEADS, 8, DN_HD), F32),
        ],
        compiler_params=pltpu.CompilerParams(
            dimension_semantics=("parallel",), vmem_limit_bytes=VMEM_LIMIT),
        name="deltanet",
    )(pm3, pm3, pm3, pm3, ps, rows4, cw, acol, dtcol, arow, dtrow, norm, ltri, utri)


def _gla_body(gq_ref, gk_ref, gv_ref, gz_ref, ps_ref, w2_ref, b2_ref, norm_ref, cmat_ref,
              o_ref, st_scr, *, n_chunks):
    c_ = CHUNK
    st_scr[...] = jnp.zeros_like(st_scr)
    rowi = lax.broadcasted_iota(jnp.int32, (c_, 1), 0)
    upper = [(rowi % (2 * s)) >= s for s in GLA_LEVELS]
    ii = lax.broadcasted_iota(jnp.int32, (2 * c_, c_), 0) % c_
    jj = lax.broadcasted_iota(jnp.int32, (2 * c_, c_), 1)
    same = [(ii // (2 * s)) == (jj // (2 * s)) for s in GLA_LEVELS]
    eye = ii == jj
    lane = lax.broadcasted_iota(jnp.int32, (c_, LANES), 1)
    first_head = lane < GLA_KD
    lane_sq = lax.broadcasted_iota(jnp.int32, (GLA_VD, LANES), 1)
    head_lanes = [lane_sq < GLA_KD, lane_sq >= GLA_KD]

    def stack_heads(t):
        return jnp.concatenate([jnp.where(first_head, t, 0.0), jnp.where(first_head, 0.0, t)], axis=0)

    def body(c, carry):
        r0 = pl.multiple_of(c * c_, c_)
        rows = pl.ds(r0, c_)
        pre = jnp.dot(ps_ref[rows, :].astype(BF16), w2_ref[...], preferred_element_type=F32) + b2_ref[...]
        gk = (jnp.minimum(pre, 0.0) - jnp.log(1.0 + jnp.exp(-jnp.abs(pre)))) * (1.0 / GLA_TAU)
        g_hi, g_lo = _split_bf16(gk)
        br = jnp.dot(cmat_ref[...], jnp.concatenate([g_hi, g_lo], axis=0), preferred_element_type=F32)
        for p in range(GLA_HEADS // 2):
            pls = slice(p * LANES, (p + 1) * LANES)
            bcum = br[0:c_, pls]
            q = gq_ref[0, rows, pls].astype(F32) * (GLA_KD ** -0.5)
            k = gk_ref[0, rows, pls].astype(F32)
            kb16 = k.astype(BF16)
            att = jnp.where(eye, lax.dot_general(stack_heads(q).astype(BF16), kb16, NT,
                                                 preferred_element_type=F32), 0.0)
            for li in range(len(GLA_LEVELS)):
                ref = br[c_ * (li + 1):c_ * (li + 2), pls]
                up = upper[li]
                e = jnp.exp(jnp.where(up, bcum - ref, ref - bcum))
                qt = jnp.where(up, q * e, 0.0)
                kt = jnp.where(up, 0.0, k * e)
                part = lax.dot_general(stack_heads(qt).astype(BF16), kt.astype(BF16), NT,
                                       preferred_element_type=F32)
                att = att + jnp.where(same[li], part, 0.0)
            blast = bcum[c_ - 1:c_, :]
            qe = (q * jnp.exp(bcum)).astype(BF16)
            ke = (k * jnp.exp(blast - bcum)).astype(BF16)
            dec = jnp.exp(blast)
            for hh in range(2):
                h = 2 * p + hh
                hs = slice(h * GLA_VD, (h + 1) * GLA_VD)
                vh = gv_ref[0, rows, hs]
                state = st_scr[h]
                o = (lax.dot_general(qe, state.astype(BF16), NT, preferred_element_type=F32)
                     + jnp.dot(att[hh * c_:(hh + 1) * c_].astype(BF16), vh, preferred_element_type=F32))
                upd = lax.dot_general(vh, ke, TN, preferred_element_type=F32)
                st_scr[h] = state * dec + jnp.where(head_lanes[hh], upd, 0.0)
                o = o * lax.rsqrt(jnp.mean(o * o, axis=-1, keepdims=True) + EPS) * norm_ref[...]
                z = gz_ref[0, rows, hs].astype(F32)
                o_ref[0, rows, hs] = (o * (z * _sigmoid(z))).astype(BF16)
        return carry

    lax.fori_loop(0, n_chunks, body, 0)


def _gla(pm3, ps, w2, b2, norm, cmat):
    b, s, _ = pm3.shape
    n_chunks = s // CHUNK
    off = 3 * ATT_W + 4 * DN_W
    full = lambda shape: pl.BlockSpec(shape, lambda bi: (0,) * len(shape))
    return pl.pallas_call(
        functools.partial(_gla_body, n_chunks=n_chunks),
        grid=(b,),
        in_specs=[
            pl.BlockSpec((1, s, GLA_KW), lambda bi: (bi, 0, off // GLA_KW)),
            pl.BlockSpec((1, s, GLA_KW), lambda bi: (bi, 0, off // GLA_KW + 1)),
            pl.BlockSpec((1, s, GLA_W), lambda bi: (bi, 0, (off + 2 * GLA_KW) // GLA_W)),
            pl.BlockSpec((1, s, GLA_W), lambda bi: (bi, 0, (off + 2 * GLA_KW) // GLA_W + 1)),
            pl.BlockSpec((s, D_SMALL), lambda bi: (bi, 0)),
            full((D_SMALL, GLA_KW)), full((1, GLA_KW)), full((1, GLA_VD)),
            full(((len(GLA_LEVELS) + 1) * CHUNK, 2 * CHUNK)),
        ],
        out_specs=pl.BlockSpec((1, s, GLA_W), lambda bi: (bi, 0, 0)),
        out_shape=jax.ShapeDtypeStruct((b, s, GLA_W), BF16),
        scratch_shapes=[pltpu.VMEM((GLA_HEADS, GLA_VD, LANES), F32)],
        compiler_params=pltpu.CompilerParams(
            dimension_semantics=("parallel",), vmem_limit_bytes=VMEM_LIMIT),
        name="gla",
    )(pm3, pm3, pm3, pm3, ps, w2, b2, norm, cmat)


def _out_body(x_ref, ya_ref, yd_ref, yg_ref, p_ref, wo_ref, postg_ref, pg_ref, pp_ref, pn_ref, o_ref):
    y = (jnp.dot(ya_ref[...], wo_ref[0:ATT_W, :], preferred_element_type=F32)
         + jnp.dot(yd_ref[...], wo_ref[ATT_W:ATT_W + DN_W, :], preferred_element_type=F32)
         + jnp.dot(yg_ref[...], wo_ref[ATT_W + DN_W:D_MIX, :], preferred_element_type=F32))
    x1 = x_ref[...] + y * lax.rsqrt(jnp.mean(y * y, axis=-1, keepdims=True) + EPS) * postg_ref[...]
    gate = _sigmoid(jnp.dot(x1.astype(BF16), pg_ref[...], preferred_element_type=F32))
    pe = jnp.dot(p_ref[...].astype(BF16), pp_ref[...], preferred_element_type=F32) * gate
    o_ref[...] = x1 + pe * lax.rsqrt(jnp.mean(pe * pe, axis=-1, keepdims=True) + EPS) * pn_ref[...]


def _out(x2, ya, yd, yg, p2, wo, postg, pg, pp, pn, *, tm):
    t = x2.shape[0]
    tile = lambda w: pl.BlockSpec((tm, w), lambda i: (i, 0))
    full = lambda shape: pl.BlockSpec(shape, lambda i: (0, 0))
    return pl.pallas_call(
        _out_body,
        grid=(t // tm,),
        in_specs=[tile(D_MODEL), tile(ATT_W), tile(DN_W), tile(GLA_W), tile(D_PLE),
                  full((D_MIX, D_MODEL)), full((1, D_MODEL)), full((D_MODEL, D_MODEL)),
                  full((D_PLE, D_MODEL)), full((1, D_MODEL))],
        out_specs=tile(D_MODEL),
        out_shape=jax.ShapeDtypeStruct((t, D_MODEL), F32),
        compiler_params=pltpu.CompilerParams(
            dimension_semantics=("parallel",), vmem_limit_bytes=VMEM_LIMIT),
        name="out",
    )(x2, ya, yd, yg, p2, wo, postg, pg, pp, pn)


def _chunk_constants():
    c = CHUNK
    ltri = np.tril(np.ones((c, c), np.float32))
    blocks = [ltri]
    idx = np.arange(c)
    for s in GLA_LEVELS:
        bnd = (idx // (2 * s)) * (2 * s) + s - 1
        blocks.append(ltri[bnd])
    cmat = np.concatenate(blocks, axis=0)
    return (jnp.asarray(np.concatenate([ltri, ltri], axis=1), BF16),
            jnp.asarray(np.concatenate([ltri.T, ltri.T], axis=0), BF16),
            jnp.asarray(np.concatenate([cmat, cmat], axis=1), BF16))


def _pick(n, pref):
    for t in pref:
        if n % t == 0:
            return t
    return n


def _layer(x, p_l, w_in, w_out, pre_g, post_g, lq1, lk1, lq2, lk2, att_subln, dn_conv, dn_a_log,
           dn_dt_bias, dn_norm, gla_w2, gla_b, gla_norm, ple_proj, ple_gate, ple_norm, layer_idx, consts):
    b, s, _ = x.shape
    t = b * s
    ltri, utri, cmat = consts
    lam_init = 0.8 - 0.6 * math.exp(-0.3 * layer_idx)

    n_big = 4 * ATT_W + 4 * DN_W
    n_small = 2 * DN_HEADS
    wm = jnp.concatenate([w_in[:, :2 * ATT_W], w_in[:, 3 * ATT_W:n_big],
                          w_in[:, n_big + n_small:n_big + n_small + 2 * GLA_KW + 2 * GLA_W]],
                         axis=1).astype(BF16)
    wvt = w_in[:, 2 * ATT_W:3 * ATT_W].T.astype(BF16)
    w_small = jnp.concatenate([w_in[:, n_big:n_big + n_small], w_in[:, -GLA_RANK:]], axis=1)
    ws = jnp.pad(w_small, ((0, 0), (0, D_SMALL - w_small.shape[1]))).astype(BF16)
    wst = w_small[:, :n_small].T.astype(BF16)

    x2 = x.reshape(t, D_MODEL)
    pm, ps, pst, vt = _proj(x2, pre_g.reshape(1, D_MODEL), wm, ws, wst, wvt, b=b, s=s,
                            tm=_pick(s, (1024, 512, 256)), tn=_pick(D_MAIN, (1664, 512)))
    pm3 = pm.reshape(b, s, D_MAIN)

    lqk = jnp.stack([lq1, lk1, lq2, lk2], axis=0)
    y_att = _attn(pm3, vt, lqk, att_subln.reshape(1, ATT_VD), tq=LANES, lam_init=lam_init)

    rows4 = pst.reshape(8, b, s // CHUNK, CHUNK).transpose(1, 2, 0, 3)
    lane_vec = lambda v, off: jnp.zeros((1, D_SMALL), F32).at[0, off:off + v.shape[0]].set(v)
    row_vec = lambda v, off: jnp.zeros((8, 1), F32).at[off:off + v.shape[0], 0].set(v)
    y_dn = _deltanet(pm3, ps, rows4, dn_conv, lane_vec(dn_a_log, DN_HEADS), lane_vec(dn_dt_bias, DN_HEADS),
                     row_vec(dn_a_log, DN_HEADS), row_vec(dn_dt_bias, DN_HEADS),
                     dn_norm.reshape(1, DN_HD), ltri, utri)

    w2 = jnp.zeros((D_SMALL, GLA_KW), F32).at[n_small:n_small + GLA_RANK].set(gla_w2).astype(BF16)
    y_gla = _gla(pm3, ps, w2, gla_b.reshape(1, GLA_KW), gla_norm.reshape(1, GLA_VD), cmat)

    out = _out(x2, y_att.reshape(t, ATT_W), y_dn.reshape(t, DN_W), y_gla.reshape(t, GLA_W),
               p_l.reshape(t, D_PLE), w_out.astype(BF16), post_g.reshape(1, D_MODEL),
               ple_gate.astype(BF16), ple_proj.astype(BF16), ple_norm.reshape(1, D_MODEL),
               tm=_pick(t, (512, 256)))
    return out.reshape(b, s, D_MODEL)


def kernel(x, p, w_in, w_out, pre_gain, post_gain, att_lq1, att_lk1, att_lq2, att_lk2, att_subln, dn_conv, dn_a_log, dn_dt_bias, dn_norm, gla_w2, gla_b, gla_norm, ple_proj, ple_gate, ple_norm):
    consts = _chunk_constants()
    for i in range(p.shape[0]):
        x = _layer(x, p[i], w_in[i], w_out[i], pre_gain[i], post_gain[i], att_lq1[i], att_lk1[i],
                   att_lq2[i], att_lk2[i], att_subln[i], dn_conv[i], dn_a_log[i], dn_dt_bias[i],
                   dn_norm[i], gla_w2[i], gla_b[i], gla_norm[i], ple_proj[i], ple_gate[i],
                   ple_norm[i], i, consts)
    return x
```

```python
import functools
import math

import numpy as np
import jax
import jax.numpy as jnp
from jax import lax
from jax.experimental import pallas as pl
from jax.experimental.pallas import tpu as pltpu

F32 = jnp.float32
BF16 = jnp.bfloat16

D_MODEL = 1024
D_PLE = 256
EPS = 1e-6
ATT_HEADS = 8
ATT_HD = 64
ATT_VD = 128
ATT_W = 1024
DN_HEADS = 4
DN_HD = 128
DN_W = 512
DN_CONV = 4
GLA_HEADS = 4
GLA_KD = 64
GLA_VD = 128
GLA_KW = 256
GLA_W = 512
GLA_RANK = 16
GLA_TAU = 16.0
CHUNK = 64
D_MIX = 2048
D_MAIN = 6656
D_SMALL = 128
LANES = 128
VMEM_LIMIT = 48 * 1024 * 1024

NT = (((1,), (1,)), ((), ()))
TN = (((0,), (0,)), ((), ()))
GLA_LEVELS = (32, 16, 8, 4, 2, 1)
LOG2E = math.log2(math.e)


def _sigmoid(x):
    return 1.0 / (1.0 + jnp.exp(-x))


def _softplus(x):
    return jnp.maximum(x, 0.0) + jnp.log(1.0 + jnp.exp(-jnp.abs(x)))


def _split_bf16(x):
    hi = x.astype(BF16)
    lo = (x - hi.astype(F32)).astype(BF16)
    return hi, lo


def _proj_body(x_ref, g_ref, wm_ref, ws_ref, wst_ref, wvt_ref, pm_ref, ps_ref, pst_ref, vt_ref, h_scr):
    @pl.when(pl.program_id(1) == 0)
    def _():
        x = x_ref[...]
        h = (x * lax.rsqrt(jnp.mean(x * x, axis=-1, keepdims=True) + EPS) * g_ref[...]).astype(BF16)
        h_scr[...] = h
        ps_ref[...] = jnp.dot(h, ws_ref[...], preferred_element_type=F32)
        pst_ref[...] = lax.dot_general(wst_ref[...], h, NT, preferred_element_type=F32)
        vt_ref[0] = lax.dot_general(wvt_ref[...], h, NT, preferred_element_type=F32).astype(BF16)

    pm_ref[...] = jnp.dot(h_scr[...], wm_ref[...], preferred_element_type=F32).astype(BF16)


def _proj(x2, g, wm, ws, wst, wvt, *, b, s, tm, tn):
    t = x2.shape[0]
    spb = s // tm
    return pl.pallas_call(
        _proj_body,
        grid=(t // tm, D_MAIN // tn),
        in_specs=[
            pl.BlockSpec((tm, D_MODEL), lambda i, j: (i, 0)),
            pl.BlockSpec((1, D_MODEL), lambda i, j: (0, 0)),
            pl.BlockSpec((D_MODEL, tn), lambda i, j: (0, j)),
            pl.BlockSpec((D_MODEL, D_SMALL), lambda i, j: (0, 0)),
            pl.BlockSpec((8, D_MODEL), lambda i, j: (0, 0)),
            pl.BlockSpec((ATT_W, D_MODEL), lambda i, j: (0, 0)),
        ],
        out_specs=[
            pl.BlockSpec((tm, tn), lambda i, j: (i, j)),
            pl.BlockSpec((tm, D_SMALL), lambda i, j: (i, 0)),
            pl.BlockSpec((8, tm), lambda i, j: (0, i)),
            pl.BlockSpec((1, ATT_W, tm), lambda i, j: (i // spb, 0, i % spb)),
        ],
        out_shape=[
            jax.ShapeDtypeStruct((t, D_MAIN), BF16),
            jax.ShapeDtypeStruct((t, D_SMALL), F32),
            jax.ShapeDtypeStruct((8, t), F32),
            jax.ShapeDtypeStruct((b, ATT_W, s), BF16),
        ],
        scratch_shapes=[pltpu.VMEM((tm, D_MODEL), BF16)],
        compiler_params=pltpu.CompilerParams(
            dimension_semantics=("parallel", "arbitrary"), vmem_limit_bytes=VMEM_LIMIT),
        name="proj",
    )(x2, g, wm, ws, wst, wvt)


def _attn_body(q_ref, k_ref, vt_ref, z_ref, lqk_ref, subln_ref, o_ref, acc_scr, *, tq, tk, n_q, lam_init):
    h = pl.program_id(1)
    sub = LANES
    nsub = tk // sub
    nch = tk // tq
    nr = 2 * tq
    lane = lax.broadcasted_iota(jnp.int32, (tq, LANES), 1)
    scale = ATT_HD ** -0.5 * LOG2E
    slope_row = LOG2E * jnp.exp2(jnp.full((1, nr), -8.0 / ATT_HEADS, F32) * (h + 1).astype(F32))
    krow = lax.broadcasted_iota(jnp.int32, (sub, nr), 0)
    qcol = lax.broadcasted_iota(jnp.int32, (sub, nr), 1)
    lag = krow - jnp.where(qcol >= tq, qcol - tq, qcol)
    bias = slope_row * krow.astype(F32)
    lqk = lqk_ref[...]
    lam = (jnp.exp(jnp.sum(lqk[0:1] * lqk[1:2], axis=-1, keepdims=True))
           - jnp.exp(jnp.sum(lqk[2:3] * lqk[3:4], axis=-1, keepdims=True)) + lam_init)

    def q_block(qb, carry):
        qbase = pl.multiple_of(qb * tk, tk)
        qs = []
        for c in range(nch):
            q = q_ref[0, pl.ds(qbase + c * tq, tq), :].astype(F32) * scale
            qs.append(jnp.concatenate([jnp.where(lane < ATT_HD, q, 0.0),
                                       jnp.where(lane >= ATT_HD, q, 0.0)], axis=0).astype(BF16))
        acc_scr[...] = jnp.zeros_like(acc_scr)

        def update(j, ms, ls, diag):
            koff = pl.multiple_of(j * tk, tk)
            k = k_ref[0, pl.ds(koff, tk), :]
            vt = vt_ref[0, :, pl.ds(koff, tk)]
            chains = range(nch)
            ns = [c + 1 if diag else nsub for c in chains]
            s_all = [lax.dot_general(k[:ns[c] * sub], qs[c], NT, preferred_element_type=F32) for c in chains]
            s, shift, ms_new = [], [], []
            for c in chains:
                sc, shc = [], []
                m_new = ms[c]
                for t in range(ns[c]):
                    st = s_all[c][t * sub:(t + 1) * sub] + bias
                    if diag and t == c:
                        st = jnp.where(lag <= 0, st, -1e30)
                    sh = slope_row * (koff - qbase + (t - c) * sub).astype(F32)
                    m_new = jnp.maximum(m_new, jnp.max(st, axis=0, keepdims=True) + sh)
                    sc.append(st)
                    shc.append(sh)
                s.append(sc)
                shift.append(shc)
                ms_new.append(m_new)
            alpha, ls_new, p = [], [], []
            for c in chains:
                a = jnp.exp2(ms[c] - ms_new[c])
                l_new = a * ls[c]
                pc = []
                for t in range(ns[c]):
                    pt = jnp.exp2(s[c][t] - (ms_new[c] - shift[c][t]))
                    l_new = l_new + jnp.sum(pt, axis=0, keepdims=True)
                    pc.append(pt.astype(BF16))
                alpha.append(a)
                ls_new.append(l_new)
                p.append(jnp.concatenate(pc, axis=0))
            pv = [jnp.dot(vt[:, :ns[c] * sub], p[c], preferred_element_type=F32) for c in chains]
            for c in chains:
                acc_scr[c] = alpha[c] * acc_scr[c] + pv[c]
            return tuple(ms_new), tuple(ls_new)

        m0 = tuple(jnp.full((1, nr), -1e30, F32) for _ in range(nch))
        l0 = tuple(jnp.zeros((1, nr), F32) for _ in range(nch))
        ms, ls = lax.fori_loop(0, qb, lambda j, c: update(j, c[0], c[1], False), (m0, l0))
        ms, ls = update(qb, ms, ls, True)

        for c in range(nch):
            rows = pl.ds(qbase + c * tq, tq)
            ot = acc_scr[c] * (1.0 / ls[c])
            o = (ot[:, :tq] - lam * ot[:, tq:]).T
            o = o * lax.rsqrt(jnp.mean(o * o, axis=-1, keepdims=True) + EPS) * subln_ref[...] * (1.0 - lam_init)
            z = z_ref[0, rows, :].astype(F32)
            o_ref[0, rows, :] = (o * (z * _sigmoid(z))).astype(BF16)
        return carry

    lax.fori_loop(0, n_q // nch, q_block, 0)


def _attn(pm3, vt, lqk, subln, *, tq, lam_init):
    b, s, _ = pm3.shape
    nq = ATT_W // LANES
    seq = lambda col: pl.BlockSpec((1, s, LANES), lambda bi, h: (bi, 0, col + h))
    tk = _pick(s, (512, 256, 128))
    return pl.pallas_call(
        functools.partial(_attn_body, tq=tq, tk=tk, n_q=s // tq, lam_init=lam_init),
        grid=(b, ATT_HEADS),
        in_specs=[
            seq(0), seq(nq),
            pl.BlockSpec((1, LANES, s), lambda bi, h: (bi, h, 0)),
            seq(2 * nq),
            pl.BlockSpec((4, ATT_HD), lambda bi, h: (0, 0)),
            pl.BlockSpec((1, ATT_VD), lambda bi, h: (0, 0)),
        ],
        out_specs=pl.BlockSpec((1, s, LANES), lambda bi, h: (bi, 0, h)),
        out_shape=jax.ShapeDtypeStruct((b, s, ATT_W), BF16),
        scratch_shapes=[pltpu.VMEM((tk // tq, ATT_VD, 2 * tq), F32)],
        compiler_params=pltpu.CompilerParams(
            dimension_semantics=("parallel", "parallel"), vmem_limit_bytes=VMEM_LIMIT),
        name="attn",
    )(pm3, pm3, vt, pm3, lqk, subln)


DN_GROUP = 2


def _dn_body(dq_ref, dk_ref, dv_ref, dz_ref, ps_ref, rows_ref, cw_ref, acol_ref, dtcol_ref,
             arow_ref, dtrow_ref, norm_ref, ltri_ref, utri_ref, o_ref,
             st_scr, u_scr, wq_scr, kd_scr, a_scr, eg_scr, *, n_chunks):
    c_ = CHUNK
    gr = DN_GROUP * c_
    ii = lax.broadcasted_iota(jnp.int32, (c_, c_), 0)
    jj = lax.broadcasted_iota(jnp.int32, (c_, c_), 1)
    tri_incl = ii >= jj
    tri_strict = ii > jj
    eye = ii == jj
    cw = cw_ref[...]
    pad = 16
    heads = range(DN_HEADS)
    lanes = [slice(h * DN_HD, (h + 1) * DN_HD) for h in heads]

    def conv_silu(win, w4):
        y = (win[pad - 3:pad - 3 + gr] * w4[0:1] + win[pad - 2:pad - 2 + gr] * w4[1:2]
             + win[pad - 1:pad - 1 + gr] * w4[2:3] + win[pad:pad + gr] * w4[3:4])
        return y * _sigmoid(y)

    def window(ref, r0, first):
        if first:
            return jnp.concatenate([jnp.zeros((pad, DN_W), F32), ref[0, 0:gr, :].astype(F32)], axis=0)
        return ref[0, pl.ds(pl.multiple_of(r0 - pad, pad), gr + pad), :].astype(F32)

    def mm(a, b):
        return jnp.dot(a.astype(BF16), b.astype(BF16), preferred_element_type=F32)

    def prep(gi, first):
        r0 = 0 if first else pl.multiple_of(gi * gr, gr)
        cq = conv_silu(window(dq_ref, r0, first), cw[:, 0:DN_W])
        ck = conv_silu(window(dk_ref, r0, first), cw[:, DN_W:2 * DN_W])
        cv = conv_silu(window(dv_ref, r0, first), cw[:, 2 * DN_W:3 * DN_W])
        ps = ps_ref[pl.ds(r0, gr), :]
        beta_all = _sigmoid(ps)
        g_all = -jnp.exp(acol_ref[...]) * _softplus(ps + dtcol_ref[...])
        g_hi, g_lo = _split_bf16(g_all)
        chains = []
        for j in range(DN_GROUP):
            cidx = gi * DN_GROUP + j
            rs = slice(j * c_, (j + 1) * c_)
            gc_all = jnp.dot(ltri_ref[...], jnp.concatenate([g_hi[rs], g_lo[rs]], axis=0),
                             preferred_element_type=F32)
            g_row = -jnp.exp(arow_ref[...]) * _softplus(rows_ref[0, cidx] + dtrow_ref[...])
            r_hi, r_lo = _split_bf16(g_row)
            gc_rows = jnp.dot(jnp.concatenate([r_hi, r_lo], axis=1), utri_ref[...],
                              preferred_element_type=F32)
            for h in heads:
                qh, kh, vh = cq[rs, lanes[h]], ck[rs, lanes[h]], cv[rs, lanes[h]]
                qn = qh * (lax.rsqrt(jnp.sum(qh * qh, axis=-1, keepdims=True) + EPS) * DN_HD ** -0.5)
                kn = kh * lax.rsqrt(jnp.sum(kh * kh, axis=-1, keepdims=True) + EPS)
                beta = beta_all[rs, h:h + 1]
                gcc = gc_all[:, 4 + h:5 + h]
                gcr = gc_rows[4 + h:5 + h, :]
                gl = gcr[:, c_ - 1:c_]
                chains.append(dict(
                    cidx=cidx, h=h, rows=pl.ds(r0 + j * c_, c_), qn=qn, kn=kn, kb=kn * beta, vb=vh * beta,
                    gcc=gcc, gl=gl, decay=jnp.exp(jnp.where(tri_incl, gcc - gcr, -1e30))))
        for ch in chains:
            ch["gram"] = lax.dot_general(jnp.concatenate([ch["kb"], ch["qn"]], axis=0).astype(BF16),
                                         ch["kn"].astype(BF16), NT, preferred_element_type=F32)
        for ch in chains:
            x = jnp.where(tri_strict, -(ch["gram"][:c_] * ch["decay"]), 0.0)
            a_scr[ch["cidx"], ch["h"]] = (ch["gram"][c_:] * ch["decay"]).astype(BF16)
            ch["acc"] = jnp.where(eye, 1.0, x)
            ch["x"] = x
        for ch in chains:
            ch["xp"] = mm(ch["x"], ch["x"])
        for _ in range(4):
            for ch in chains:
                ch["both"] = mm(jnp.concatenate([ch["acc"], ch["xp"]], axis=0), ch["xp"])
            for ch in chains:
                ch["acc"] = ch["acc"] + ch["both"][:c_]
                ch["xp"] = ch["both"][c_:]
        for ch in chains:
            ch["last"] = mm(ch["acc"], ch["xp"])
        for ch in chains:
            t_inv = ch["acc"] + ch["last"]
            gexp = jnp.exp(ch["gcc"])
            ch["gexp"] = gexp
            ch["uw"] = mm(t_inv, jnp.concatenate([ch["vb"], ch["kb"] * gexp], axis=1))
        for ch in chains:
            cidx, h, rows = ch["cidx"], ch["h"], ch["rows"]
            u_scr[rows, lanes[h]] = ch["uw"][:, :DN_HD]
            wq_scr[cidx, 0:c_, lanes[h]] = ch["uw"][:, DN_HD:].astype(BF16)
            wq_scr[cidx, c_:2 * c_, lanes[h]] = (ch["qn"] * ch["gexp"]).astype(BF16)
            kd_scr[rows, lanes[h]] = (ch["kn"] * jnp.exp(ch["gl"] - ch["gcc"])).astype(BF16)
            eg_scr[cidx, h] = jnp.broadcast_to(jnp.exp(ch["gl"]), (8, DN_HD))

    prep(0, True)

    def prep_body(gi, carry):
        prep(gi, False)
        return carry

    lax.fori_loop(1, n_chunks // DN_GROUP, prep_body, 0)

    st_scr[...] = jnp.zeros_like(st_scr)

    def scan(ci, carry):
        states = [st_scr[h] for h in heads]
        for u in range(DN_GROUP):
            c = ci * DN_GROUP + u
            rows = pl.ds(pl.multiple_of(c * c_, c_), c_)
            wq = [jnp.dot(wq_scr[c, :, lanes[h]], states[h].astype(BF16), preferred_element_type=F32)
                  for h in heads]
            v_new = [(u_scr[rows, lanes[h]] - wq[h][:c_]).astype(BF16) for h in heads]
            kv = [lax.dot_general(kd_scr[rows, lanes[h]], v_new[h], TN, preferred_element_type=F32)
                  for h in heads]
            o = [wq[h][c_:] + jnp.dot(a_scr[c, h], v_new[h], preferred_element_type=F32) for h in heads]
            states = [states[h] * eg_scr[c, h][0:1, :] + kv[h] for h in heads]
            for h in heads:
                on = o[h] * lax.rsqrt(jnp.mean(o[h] * o[h], axis=-1, keepdims=True) + EPS) * norm_ref[...]
                z = dz_ref[0, rows, lanes[h]].astype(F32)
                o_ref[0, rows, lanes[h]] = (on * (z * _sigmoid(z))).astype(BF16)
        for h in heads:
            st_scr[h] = states[h]
        return carry

    lax.fori_loop(0, n_chunks // DN_GROUP, scan, 0)


def _deltanet(pm3, ps, rows4, cw, acol, dtcol, arow, dtrow, norm, ltri, utri):
    b, s, _ = pm3.shape
    n_chunks = s // CHUNK
    assert n_chunks % DN_GROUP == 0
    base = 3 * ATT_W // DN_W
    full = lambda shape: pl.BlockSpec(shape, lambda bi: (0,) * len(shape))
    return pl.pallas_call(
        functools.partial(_dn_body, n_chunks=n_chunks),
        grid=(b,),
        in_specs=[
            pl.BlockSpec((1, s, DN_W), lambda bi: (bi, 0, base)),
            pl.BlockSpec((1, s, DN_W), lambda bi: (bi, 0, base + 1)),
            pl.BlockSpec((1, s, DN_W), lambda bi: (bi, 0, base + 2)),
            pl.BlockSpec((1, s, DN_W), lambda bi: (bi, 0, base + 3)),
            pl.BlockSpec((s, D_SMALL), lambda bi: (bi, 0)),
            pl.BlockSpec((1, n_chunks, 8, CHUNK), lambda bi: (bi, 0, 0, 0)),
            full((DN_CONV, 3 * DN_W)),
            full((1, D_SMALL)), full((1, D_SMALL)), full((8, 1)), full((8, 1)),
            full((1, DN_HD)),
            full((CHUNK, 2 * CHUNK)), full((2 * CHUNK, CHUNK)),
        ],
        out_specs=pl.BlockSpec((1, s, DN_W), lambda bi: (bi, 0, 0)),
        out_shape=jax.ShapeDtypeStruct((b, s, DN_W), BF16),
        scratch_shapes=[
            pltpu.VMEM((DN_HEADS, DN_HD, DN_HD), F32),
            pltpu.VMEM((s, DN_W), F32),
            pltpu.VMEM((n_chunks, 2 * CHUNK, DN_W), BF16),
            pltpu.VMEM((s, DN_W), BF16),
            pltpu.VMEM((n_chunks, DN_HEADS, CHUNK, CHUNK), BF16),
            pltpu.VMEM((n_chunks, DN_HEADS, 8, DN_HD), F32),
        ],
        compiler_params=pltpu.CompilerParams(
            dimension_semantics=("parallel",), vmem_limit_bytes=VMEM_LIMIT),
        name="deltanet",
    )(pm3, pm3, pm3, pm3, ps, rows4, cw, acol, dtcol, arow, dtrow, norm, ltri, utri)


GLA_GROUP = 2


def _gla_body(gq_ref, gk_ref, gv_ref, gz_ref, ps_ref, w2_ref, b2_ref, norm_ref, cmat_ref,
              o_ref, st_scr, att_scr, qe_scr, ke_scr, dec_scr, *, n_chunks):
    c_ = CHUNK
    rowi = lax.broadcasted_iota(jnp.int32, (c_, 1), 0)
    upper = [(rowi % (2 * s)) >= s for s in GLA_LEVELS]
    ii = lax.broadcasted_iota(jnp.int32, (2 * c_, c_), 0) % c_
    jj = lax.broadcasted_iota(jnp.int32, (2 * c_, c_), 1)
    same = [(ii // (2 * s)) == (jj // (2 * s)) for s in GLA_LEVELS]
    eye = ii == jj
    lane = lax.broadcasted_iota(jnp.int32, (c_, LANES), 1)
    first_head = lane < GLA_KD
    lane_sq = lax.broadcasted_iota(jnp.int32, (GLA_VD, LANES), 1)
    head_lanes = [lane_sq < GLA_KD, lane_sq >= GLA_KD]

    def stack_heads(t):
        return jnp.concatenate([jnp.where(first_head, t, 0.0), jnp.where(first_head, 0.0, t)], axis=0)

    pairs = range(GLA_HEADS // 2)
    pair_lanes = [slice(p * LANES, (p + 1) * LANES) for p in pairs]
    gr = GLA_GROUP * c_

    def prep(gi, carry):
        r0 = pl.multiple_of(gi * gr, gr)
        pre = (jnp.dot(ps_ref[pl.ds(r0, gr), :].astype(BF16), w2_ref[...], preferred_element_type=F32)
               + b2_ref[...])
        gk = (jnp.minimum(pre, 0.0) - jnp.log(1.0 + jnp.exp(-jnp.abs(pre)))) * (1.0 / GLA_TAU)
        g_hi, g_lo = _split_bf16(gk)
        chains = []
        for j in range(GLA_GROUP):
            rs = slice(j * c_, (j + 1) * c_)
            rows = pl.ds(r0 + j * c_, c_)
            br = jnp.dot(cmat_ref[...], jnp.concatenate([g_hi[rs], g_lo[rs]], axis=0),
                         preferred_element_type=F32)
            for p in pairs:
                q = gq_ref[0, rows, pair_lanes[p]].astype(F32) * (GLA_KD ** -0.5)
                k = gk_ref[0, rows, pair_lanes[p]].astype(F32)
                chains.append(dict(cidx=gi * GLA_GROUP + j, p=p, rows=rows, br=br, q=q, k=k,
                                   bcum=br[0:c_, pair_lanes[p]]))
        for ch in chains:
            ch["part"] = lax.dot_general(stack_heads(ch["q"]).astype(BF16), ch["k"].astype(BF16), NT,
                                         preferred_element_type=F32)
        for ch in chains:
            ch["att"] = jnp.where(eye, ch["part"], 0.0)
        for li in range(len(GLA_LEVELS)):
            up = upper[li]
            for ch in chains:
                ref = ch["br"][c_ * (li + 1):c_ * (li + 2), pair_lanes[ch["p"]]]
                e = jnp.exp(jnp.where(up, ch["bcum"] - ref, ref - ch["bcum"]))
                qt = jnp.where(up, ch["q"] * e, 0.0)
                kt = jnp.where(up, 0.0, ch["k"] * e)
                ch["part"] = lax.dot_general(stack_heads(qt).astype(BF16), kt.astype(BF16), NT,
                                             preferred_element_type=F32)
            for ch in chains:
                ch["att"] = ch["att"] + jnp.where(same[li], ch["part"], 0.0)
        for ch in chains:
            cidx, p, rows, bcum = ch["cidx"], ch["p"], ch["rows"], ch["bcum"]
            blast = bcum[c_ - 1:c_, :]
            att_scr[cidx, 2 * p] = ch["att"][:c_].astype(BF16)
            att_scr[cidx, 2 * p + 1] = ch["att"][c_:].astype(BF16)
            qe_scr[rows, pair_lanes[p]] = (ch["q"] * jnp.exp(bcum)).astype(BF16)
            ke_scr[rows, pair_lanes[p]] = (ch["k"] * jnp.exp(blast - bcum)).astype(BF16)
            dec_scr[cidx, p] = jnp.broadcast_to(jnp.exp(blast), (8, LANES))
        return carry

    lax.fori_loop(0, n_chunks // GLA_GROUP, prep, 0)

    st_scr[...] = jnp.zeros_like(st_scr)
    heads = range(GLA_HEADS)
    head_cols = [slice(h * GLA_VD, (h + 1) * GLA_VD) for h in heads]

    def scan(ci, carry):
        states = [st_scr[h] for h in heads]
        for u in range(GLA_GROUP):
            c = ci * GLA_GROUP + u
            rows = pl.ds(pl.multiple_of(c * c_, c_), c_)
            qe = [qe_scr[rows, pair_lanes[p]] for p in pairs]
            ke = [ke_scr[rows, pair_lanes[p]] for p in pairs]
            vh = [gv_ref[0, rows, head_cols[h]] for h in heads]
            upd = [lax.dot_general(vh[h], ke[h // 2], TN, preferred_element_type=F32) for h in heads]
            intra = [jnp.dot(att_scr[c, h], vh[h], preferred_element_type=F32) for h in heads]
            inter = [lax.dot_general(qe[h // 2], states[h].astype(BF16), NT, preferred_element_type=F32)
                     for h in heads]
            states = [states[h] * dec_scr[c, h // 2][0:1, :] + jnp.where(head_lanes[h % 2], upd[h], 0.0)
                      for h in heads]
            for h in heads:
                o = inter[h] + intra[h]
                o = o * lax.rsqrt(jnp.mean(o * o, axis=-1, keepdims=True) + EPS) * norm_ref[...]
                z = gz_ref[0, rows, head_cols[h]].astype(F32)
                o_ref[0, rows, head_cols[h]] = (o * (z * _sigmoid(z))).astype(BF16)
        for h in heads:
            st_scr[h] = states[h]
        return carry

    lax.fori_loop(0, n_chunks // GLA_GROUP, scan, 0)


def _gla(pm3, ps, w2, b2, norm, cmat):
    b, s, _ = pm3.shape
    n_chunks = s // CHUNK
    off = 3 * ATT_W + 4 * DN_W
    full = lambda shape: pl.BlockSpec(shape, lambda bi: (0,) * len(shape))
    return pl.pallas_call(
        functools.partial(_gla_body, n_chunks=n_chunks),
        grid=(b,),
        in_specs=[
            pl.BlockSpec((1, s, GLA_KW), lambda bi: (bi, 0, off // GLA_KW)),
            pl.BlockSpec((1, s, GLA_KW), lambda bi: (bi, 0, off // GLA_KW + 1)),
            pl.BlockSpec((1, s, GLA_W), lambda bi: (bi, 0, (off + 2 * GLA_KW) // GLA_W)),
            pl.BlockSpec((1, s, GLA_W), lambda bi: (bi, 0, (off + 2 * GLA_KW) // GLA_W + 1)),
            pl.BlockSpec((s, D_SMALL), lambda bi: (bi, 0)),
            full((D_SMALL, GLA_KW)), full((1, GLA_KW)), full((1, GLA_VD)),
            full(((len(GLA_LEVELS) + 1) * CHUNK, 2 * CHUNK)),
        ],
        out_specs=pl.BlockSpec((1, s, GLA_W), lambda bi: (bi, 0, 0)),
        out_shape=jax.ShapeDtypeStruct((b, s, GLA_W), BF16),
        scratch_shapes=[
            pltpu.VMEM((GLA_HEADS, GLA_VD, LANES), F32),
            pltpu.VMEM((n_chunks, GLA_HEADS, CHUNK, CHUNK), BF16),
            pltpu.VMEM((s, GLA_KW), BF16),
            pltpu.VMEM((s, GLA_KW), BF16),
            pltpu.VMEM((n_chunks, GLA_HEADS // 2, 8, LANES), F32),
        ],
        compiler_params=pltpu.CompilerParams(
            dimension_semantics=("parallel",), vmem_limit_bytes=VMEM_LIMIT),
        name="gla",
    )(pm3, pm3, pm3, pm3, ps, w2, b2, norm, cmat)


def _out_body(x_ref, ya_ref, yd_ref, yg_ref, p_ref, wo_ref, postg_ref, pg_ref, pp_ref, pn_ref, o_ref):
    y = (jnp.dot(ya_ref[...], wo_ref[0:ATT_W, :], preferred_element_type=F32)
         + jnp.dot(yd_ref[...], wo_ref[ATT_W:ATT_W + DN_W, :], preferred_element_type=F32)
         + jnp.dot(yg_ref[...], wo_ref[ATT_W + DN_W:D_MIX, :], preferred_element_type=F32))
    x1 = x_ref[...] + y * lax.rsqrt(jnp.mean(y * y, axis=-1, keepdims=True) + EPS) * postg_ref[...]
    gate = _sigmoid(jnp.dot(x1.astype(BF16), pg_ref[...], preferred_element_type=F32))
    pe = jnp.dot(p_ref[...].astype(BF16), pp_ref[...], preferred_element_type=F32) * gate
    o_ref[...] = x1 + pe * lax.rsqrt(jnp.mean(pe * pe, axis=-1, keepdims=True) + EPS) * pn_ref[...]


def _out(x2, ya, yd, yg, p2, wo, postg, pg, pp, pn, *, tm):
    t = x2.shape[0]
    tile = lambda w: pl.BlockSpec((tm, w), lambda i: (i, 0))
    full = lambda shape: pl.BlockSpec(shape, lambda i: (0, 0))
    return pl.pallas_call(
        _out_body,
        grid=(t // tm,),
        in_specs=[tile(D_MODEL), tile(ATT_W), tile(DN_W), tile(GLA_W), tile(D_PLE),
                  full((D_MIX, D_MODEL)), full((1, D_MODEL)), full((D_MODEL, D_MODEL)),
                  full((D_PLE, D_MODEL)), full((1, D_MODEL))],
        out_specs=tile(D_MODEL),
        out_shape=jax.ShapeDtypeStruct((t, D_MODEL), F32),
        compiler_params=pltpu.CompilerParams(
            dimension_semantics=("parallel",), vmem_limit_bytes=VMEM_LIMIT),
        name="out",
    )(x2, ya, yd, yg, p2, wo, postg, pg, pp, pn)


def _chunk_constants():
    c = CHUNK
    ltri = np.tril(np.ones((c, c), np.float32))
    blocks = [ltri]
    idx = np.arange(c)
    for s in GLA_LEVELS:
        bnd = (idx // (2 * s)) * (2 * s) + s - 1
        blocks.append(ltri[bnd])
    cmat = np.concatenate(blocks, axis=0)
    return (jnp.asarray(np.concatenate([ltri, ltri], axis=1), BF16),
            jnp.asarray(np.concatenate([ltri.T, ltri.T], axis=0), BF16),
            jnp.asarray(np.concatenate([cmat, cmat], axis=1), BF16))


def _pick(n, pref):
    for t in pref:
        if n % t == 0:
            return t
    return n


def _layer(x, p_l, w_in, w_out, pre_g, post_g, lq1, lk1, lq2, lk2, att_subln, dn_conv, dn_a_log,
           dn_dt_bias, dn_norm, gla_w2, gla_b, gla_norm, ple_proj, ple_gate, ple_norm, layer_idx, consts):
    b, s, _ = x.shape
    t = b * s
    ltri, utri, cmat = consts
    lam_init = 0.8 - 0.6 * math.exp(-0.3 * layer_idx)

    n_big = 4 * ATT_W + 4 * DN_W
    n_small = 2 * DN_HEADS
    wm = jnp.concatenate([w_in[:, :2 * ATT_W], w_in[:, 3 * ATT_W:n_big],
                          w_in[:, n_big + n_small:n_big + n_small + 2 * GLA_KW + 2 * GLA_W]],
                         axis=1).astype(BF16)
    wvt = w_in[:, 2 * ATT_W:3 * ATT_W].T.astype(BF16)
    w_small = jnp.concatenate([w_in[:, n_big:n_big + n_small], w_in[:, -GLA_RANK:]], axis=1)
    ws = jnp.pad(w_small, ((0, 0), (0, D_SMALL - w_small.shape[1]))).astype(BF16)
    wst = w_small[:, :n_small].T.astype(BF16)

    x2 = x.reshape(t, D_MODEL)
    pm, ps, pst, vt = _proj(x2, pre_g.reshape(1, D_MODEL), wm, ws, wst, wvt, b=b, s=s,
                            tm=_pick(s, (1024, 512, 256)), tn=_pick(D_MAIN, (1664, 512)))
    pm3 = pm.reshape(b, s, D_MAIN)

    lqk = jnp.stack([lq1, lk1, lq2, lk2], axis=0)
    y_att = _attn(pm3, vt, lqk, att_subln.reshape(1, ATT_VD), tq=LANES, lam_init=lam_init)

    rows4 = pst.reshape(8, b, s // CHUNK, CHUNK).transpose(1, 2, 0, 3)
    lane_vec = lambda v, off: jnp.zeros((1, D_SMALL), F32).at[0, off:off + v.shape[0]].set(v)
    row_vec = lambda v, off: jnp.zeros((8, 1), F32).at[off:off + v.shape[0], 0].set(v)
    y_dn = _deltanet(pm3, ps, rows4, dn_conv, lane_vec(dn_a_log, DN_HEADS), lane_vec(dn_dt_bias, DN_HEADS),
                     row_vec(dn_a_log, DN_HEADS), row_vec(dn_dt_bias, DN_HEADS),
                     dn_norm.reshape(1, DN_HD), ltri, utri)

    w2 = jnp.zeros((D_SMALL, GLA_KW), F32).at[n_small:n_small + GLA_RANK].set(gla_w2).astype(BF16)
    y_gla = _gla(pm3, ps, w2, gla_b.reshape(1, GLA_KW), gla_norm.reshape(1, GLA_VD), cmat)

    out = _out(x2, y_att.reshape(t, ATT_W), y_dn.reshape(t, DN_W), y_gla.reshape(t, GLA_W),
               p_l.reshape(t, D_PLE), w_out.astype(BF16), post_g.reshape(1, D_MODEL),
               ple_gate.astype(BF16), ple_proj.astype(BF16), ple_norm.reshape(1, D_MODEL),
               tm=_pick(t, (512, 256)))
    return out.reshape(b, s, D_MODEL)


def kernel(x, p, w_in, w_out, pre_gain, post_gain, att_lq1, att_lk1, att_lq2, att_lk2, att_subln, dn_conv, dn_a_log, dn_dt_bias, dn_norm, gla_w2, gla_b, gla_norm, ple_proj, ple_gate, ple_norm):
    consts = _chunk_constants()
    for i in range(p.shape[0]):
        x = _layer(x, p[i], w_in[i], w_out[i], pre_gain[i], post_gain[i], att_lq1[i], att_lk1[i],
                   att_lq2[i], att_lk2[i], att_subln[i], dn_conv[i], dn_a_log[i], dn_dt_bias[i],
                   dn_norm[i], gla_w2[i], gla_b[i], gla_norm[i], ple_proj[i], ple_gate[i],
                   ple_norm[i], i, consts)
    return x
```

```python
import functools
import math

import numpy as np
import jax
import jax.numpy as jnp
from jax import lax
from jax.experimental import pallas as pl
from jax.experimental.pallas import tpu as pltpu

F32 = jnp.float32
BF16 = jnp.bfloat16

D_MODEL = 1024
D_PLE = 256
EPS = 1e-6
ATT_HEADS = 8
ATT_HD = 64
ATT_VD = 128
ATT_W = 1024
DN_HEADS = 4
DN_HD = 128
DN_W = 512
DN_CONV = 4
GLA_HEADS = 4
GLA_KD = 64
GLA_VD = 128
GLA_KW = 256
GLA_W = 512
GLA_RANK = 16
GLA_TAU = 16.0
CHUNK = 64
D_MIX = 2048
D_MAIN = 6656
D_SMALL = 128
LANES = 128
VMEM_LIMIT = 48 * 1024 * 1024

NT = (((1,), (1,)), ((), ()))
TN = (((0,), (0,)), ((), ()))
GLA_LEVELS = (32, 16, 8, 4, 2, 1)
LOG2E = math.log2(math.e)
ATT_ONES = 16


def _sigmoid(x):
    return 1.0 / (1.0 + jnp.exp(-x))


def _softplus(x):
    return jnp.maximum(x, 0.0) + jnp.log(1.0 + jnp.exp(-jnp.abs(x)))


def _split_bf16(x):
    hi = x.astype(BF16)
    lo = (x - hi.astype(F32)).astype(BF16)
    return hi, lo


def _proj_body(x_ref, g_ref, wm_ref, ws_ref, wst_ref, wvt_ref, pm_ref, ps_ref, pst_ref, vt_ref, h_scr):
    @pl.when(pl.program_id(1) == 0)
    def _():
        x = x_ref[...]
        h = (x * lax.rsqrt(jnp.mean(x * x, axis=-1, keepdims=True) + EPS) * g_ref[...]).astype(BF16)
        h_scr[...] = h
        ps_ref[...] = jnp.dot(h, ws_ref[...], preferred_element_type=F32)
        pst_ref[...] = lax.dot_general(wst_ref[...], h, NT, preferred_element_type=F32)
        vt_ref[0] = lax.dot_general(wvt_ref[...], h, NT, preferred_element_type=F32).astype(BF16)

    pm_ref[...] = jnp.dot(h_scr[...], wm_ref[...], preferred_element_type=F32).astype(BF16)


def _proj(x2, g, wm, ws, wst, wvt, *, b, s, tm, tn):
    t = x2.shape[0]
    spb = s // tm
    return pl.pallas_call(
        _proj_body,
        grid=(t // tm, D_MAIN // tn),
        in_specs=[
            pl.BlockSpec((tm, D_MODEL), lambda i, j: (i, 0)),
            pl.BlockSpec((1, D_MODEL), lambda i, j: (0, 0)),
            pl.BlockSpec((D_MODEL, tn), lambda i, j: (0, j)),
            pl.BlockSpec((D_MODEL, D_SMALL), lambda i, j: (0, 0)),
            pl.BlockSpec((8, D_MODEL), lambda i, j: (0, 0)),
            pl.BlockSpec((ATT_W, D_MODEL), lambda i, j: (0, 0)),
        ],
        out_specs=[
            pl.BlockSpec((tm, tn), lambda i, j: (i, j)),
            pl.BlockSpec((tm, D_SMALL), lambda i, j: (i, 0)),
            pl.BlockSpec((8, tm), lambda i, j: (0, i)),
            pl.BlockSpec((1, ATT_W, tm), lambda i, j: (i // spb, 0, i % spb)),
        ],
        out_shape=[
            jax.ShapeDtypeStruct((t, D_MAIN), BF16),
            jax.ShapeDtypeStruct((t, D_SMALL), F32),
            jax.ShapeDtypeStruct((8, t), F32),
            jax.ShapeDtypeStruct((b, ATT_W, s), BF16),
        ],
        scratch_shapes=[pltpu.VMEM((tm, D_MODEL), BF16)],
        compiler_params=pltpu.CompilerParams(
            dimension_semantics=("parallel", "arbitrary"), vmem_limit_bytes=VMEM_LIMIT),
        name="proj",
    )(x2, g, wm, ws, wst, wvt)


def _attn_body(q_ref, k_ref, vt_ref, z_ref, lqk_ref, subln_ref, o_ref, acc_scr, *, tq, tk, n_q, lam_init):
    h = pl.program_id(1)
    sub = LANES
    nsub = tk // sub
    nch = tk // tq
    nr = 2 * tq
    lane = lax.broadcasted_iota(jnp.int32, (tq, LANES), 1)
    scale = ATT_HD ** -0.5 * LOG2E
    slope_row = LOG2E * jnp.exp2(jnp.full((1, nr), -8.0 / ATT_HEADS, F32) * (h + 1).astype(F32))
    krow = lax.broadcasted_iota(jnp.int32, (sub, nr), 0)
    qcol = lax.broadcasted_iota(jnp.int32, (sub, nr), 1)
    lag = krow - jnp.where(qcol >= tq, qcol - tq, qcol)
    bias = slope_row * krow.astype(F32)
    ones_rows = jnp.ones((ATT_ONES, tk), BF16)
    lqk = lqk_ref[...]
    lam = (jnp.exp(jnp.sum(lqk[0:1] * lqk[1:2], axis=-1, keepdims=True))
           - jnp.exp(jnp.sum(lqk[2:3] * lqk[3:4], axis=-1, keepdims=True)) + lam_init)

    def q_block(qb, carry):
        qbase = pl.multiple_of(qb * tk, tk)
        qs = []
        for c in range(nch):
            q = q_ref[0, pl.ds(qbase + c * tq, tq), :].astype(F32) * scale
            qs.append(jnp.concatenate([jnp.where(lane < ATT_HD, q, 0.0),
                                       jnp.where(lane >= ATT_HD, q, 0.0)], axis=0).astype(BF16))
        acc_scr[...] = jnp.zeros_like(acc_scr)

        def update(j, ms, diag):
            koff = pl.multiple_of(j * tk, tk)
            k = k_ref[0, pl.ds(koff, tk), :]
            vt = jnp.concatenate([vt_ref[0, :, pl.ds(koff, tk)], ones_rows], axis=0)
            chains = range(nch)
            ns = [c + 1 if diag else nsub for c in chains]
            s_all = [lax.dot_general(k[:ns[c] * sub], qs[c], NT, preferred_element_type=F32) for c in chains]
            s, shift, ms_new = [], [], []
            for c in chains:
                sc, shc = [], []
                m_new = ms[c]
                for t in range(ns[c]):
                    st = s_all[c][t * sub:(t + 1) * sub] + bias
                    if diag and t == c:
                        st = jnp.where(lag <= 0, st, -1e30)
                    sh = slope_row * (koff - qbase + (t - c) * sub).astype(F32)
                    m_new = jnp.maximum(m_new, jnp.max(st, axis=0, keepdims=True) + sh)
                    sc.append(st)
                    shc.append(sh)
                s.append(sc)
                shift.append(shc)
                ms_new.append(m_new)
            p = [jnp.concatenate([jnp.exp2((s[c][t] - (ms_new[c] - shift[c][t])).astype(BF16))
                                  for t in range(ns[c])], axis=0) for c in chains]
            pv = [jnp.dot(vt[:, :ns[c] * sub], p[c], preferred_element_type=F32) for c in chains]
            for c in chains:
                acc_scr[c] = jnp.exp2(ms[c] - ms_new[c]) * acc_scr[c] + pv[c]
            return tuple(ms_new)

        m0 = tuple(jnp.full((1, nr), -1e30, F32) for _ in range(nch))
        ms = lax.fori_loop(0, qb, lambda j, c: update(j, c, False), m0)
        update(qb, ms, True)

        for c in range(nch):
            rows = pl.ds(qbase + c * tq, tq)
            acc = acc_scr[c]
            ot = acc[:ATT_VD] * (1.0 / acc[ATT_VD:ATT_VD + 1])
            o = (ot[:, :tq] - lam * ot[:, tq:]).T
            o = o * lax.rsqrt(jnp.mean(o * o, axis=-1, keepdims=True) + EPS) * subln_ref[...] * (1.0 - lam_init)
            z = z_ref[0, rows, :].astype(F32)
            o_ref[0, rows, :] = (o * (z * _sigmoid(z))).astype(BF16)
        return carry

    lax.fori_loop(0, n_q // nch, q_block, 0)


def _attn(pm3, vt, lqk, subln, *, tq, lam_init):
    b, s, _ = pm3.shape
    nq = ATT_W // LANES
    seq = lambda col: pl.BlockSpec((1, s, LANES), lambda bi, h: (bi, 0, col + h))
    tk = _pick(s, (512, 256, 128))
    return pl.pallas_call(
        functools.partial(_attn_body, tq=tq, tk=tk, n_q=s // tq, lam_init=lam_init),
        grid=(b, ATT_HEADS),
        in_specs=[
            seq(0), seq(nq),
            pl.BlockSpec((1, LANES, s), lambda bi, h: (bi, h, 0)),
            seq(2 * nq),
            pl.BlockSpec((4, ATT_HD), lambda bi, h: (0, 0)),
            pl.BlockSpec((1, ATT_VD), lambda bi, h: (0, 0)),
        ],
        out_specs=pl.BlockSpec((1, s, LANES), lambda bi, h: (bi, 0, h)),
        out_shape=jax.ShapeDtypeStruct((b, s, ATT_W), BF16),
        scratch_shapes=[pltpu.VMEM((tk // tq, ATT_VD + ATT_ONES, 2 * tq), F32)],
        compiler_params=pltpu.CompilerParams(
            dimension_semantics=("parallel", "parallel"), vmem_limit_bytes=VMEM_LIMIT),
        name="attn",
    )(pm3, pm3, vt, pm3, lqk, subln)


DN_GROUP = 2


def _dn_body(dq_ref, dk_ref, dv_ref, dz_ref, ps_ref, rows_ref, cw_ref, acol_ref, dtcol_ref,
             arow_ref, dtrow_ref, norm_ref, ltri_ref, utri_ref, shift_ref, o_ref,
             st_scr, u_scr, wq_scr, kd_scr, a_scr, eg_scr, *, n_chunks):
    c_ = CHUNK
    gr = DN_GROUP * c_
    ii = lax.broadcasted_iota(jnp.int32, (c_, c_), 0)
    jj = lax.broadcasted_iota(jnp.int32, (c_, c_), 1)
    tri_incl = ii >= jj
    tri_strict = ii > jj
    eye = ii == jj
    cw = cw_ref[...]
    heads = range(DN_HEADS)
    lanes = [slice(h * DN_HD, (h + 1) * DN_HD) for h in heads]

    def conv_silu(win, w4):
        delayed = jnp.dot(shift_ref[...], win, preferred_element_type=F32)
        y = (delayed[0:gr] * w4[0:1] + delayed[gr:2 * gr] * w4[1:2] + delayed[2 * gr:3 * gr] * w4[2:3]
             + win[gr:2 * gr].astype(F32) * w4[3:4])
        return y * _sigmoid(y)

    def window(ref, r0, first):
        if first:
            return jnp.concatenate([jnp.zeros((gr, DN_W), BF16), ref[0, 0:gr, :]], axis=0)
        return ref[0, pl.ds(pl.multiple_of(r0 - gr, gr), 2 * gr), :]

    def mm(a, b):
        return jnp.dot(a.astype(BF16), b.astype(BF16), preferred_element_type=F32)

    def prep(gi, first):
        r0 = 0 if first else pl.multiple_of(gi * gr, gr)
        cq = conv_silu(window(dq_ref, r0, first), cw[:, 0:DN_W])
        ck = conv_silu(window(dk_ref, r0, first), cw[:, DN_W:2 * DN_W])
        cv = conv_silu(window(dv_ref, r0, first), cw[:, 2 * DN_W:3 * DN_W])
        ps = ps_ref[pl.ds(r0, gr), :]
        beta_all = _sigmoid(ps)
        g_all = -jnp.exp(acol_ref[...]) * _softplus(ps + dtcol_ref[...])
        g_hi, g_lo = _split_bf16(g_all)
        chains = []
        for j in range(DN_GROUP):
            cidx = gi * DN_GROUP + j
            rs = slice(j * c_, (j + 1) * c_)
            gc_all = jnp.dot(ltri_ref[...], jnp.concatenate([g_hi[rs], g_lo[rs]], axis=0),
                             preferred_element_type=F32)
            g_row = -jnp.exp(arow_ref[...]) * _softplus(rows_ref[0, cidx] + dtrow_ref[...])
            r_hi, r_lo = _split_bf16(g_row)
            gc_rows = jnp.dot(jnp.concatenate([r_hi, r_lo], axis=1), utri_ref[...],
                              preferred_element_type=F32)
            for h in heads:
                qh, kh, vh = cq[rs, lanes[h]], ck[rs, lanes[h]], cv[rs, lanes[h]]
                qn = qh * (lax.rsqrt(jnp.sum(qh * qh, axis=-1, keepdims=True) + EPS) * DN_HD ** -0.5)
                kn = kh * lax.rsqrt(jnp.sum(kh * kh, axis=-1, keepdims=True) + EPS)
                beta = beta_all[rs, h:h + 1]
                gcc = gc_all[:, 4 + h:5 + h]
                gcr = gc_rows[4 + h:5 + h, :]
                gl = gcr[:, c_ - 1:c_]
                chains.append(dict(
                    cidx=cidx, h=h, rows=pl.ds(r0 + j * c_, c_), qn=qn, kn=kn, kb=kn * beta, vb=vh * beta,
                    gcc=gcc, gl=gl, decay=jnp.exp(jnp.where(tri_incl, gcc - gcr, -1e30))))
        for ch in chains:
            ch["gram"] = lax.dot_general(jnp.concatenate([ch["kb"], ch["qn"]], axis=0).astype(BF16),
                                         ch["kn"].astype(BF16), NT, preferred_element_type=F32)
        for ch in chains:
            x = jnp.where(tri_strict, -(ch["gram"][:c_] * ch["decay"]), 0.0)
            a_scr[ch["cidx"], ch["h"]] = (ch["gram"][c_:] * ch["decay"]).astype(BF16)
            ch["acc"] = jnp.where(eye, 1.0, x)
            ch["x"] = x
        for ch in chains:
            ch["xp"] = mm(ch["x"], ch["x"])
        for _ in range(4):
            for ch in chains:
                ch["both"] = mm(jnp.concatenate([ch["acc"], ch["xp"]], axis=0), ch["xp"])
            for ch in chains:
                ch["acc"] = ch["acc"] + ch["both"][:c_]
                ch["xp"] = ch["both"][c_:]
        for ch in chains:
            ch["last"] = mm(ch["acc"], ch["xp"])
        for ch in chains:
            t_inv = ch["acc"] + ch["last"]
            gexp = jnp.exp(ch["gcc"])
            ch["gexp"] = gexp
            ch["uw"] = mm(t_inv, jnp.concatenate([ch["vb"], ch["kb"] * gexp], axis=1))
        for ch in chains:
            cidx, h, rows = ch["cidx"], ch["h"], ch["rows"]
            u_scr[rows, lanes[h]] = ch["uw"][:, :DN_HD]
            wq_scr[cidx, 0:c_, lanes[h]] = ch["uw"][:, DN_HD:].astype(BF16)
            wq_scr[cidx, c_:2 * c_, lanes[h]] = (ch["qn"] * ch["gexp"]).astype(BF16)
            kd_scr[rows, lanes[h]] = (ch["kn"] * jnp.exp(ch["gl"] - ch["gcc"])).astype(BF16)
            eg_scr[cidx, h] = jnp.broadcast_to(jnp.exp(ch["gl"]), (8, DN_HD))

    prep(0, True)

    def prep_body(gi, carry):
        prep(gi, False)
        return carry

    lax.fori_loop(1, n_chunks // DN_GROUP, prep_body, 0)

    st_scr[...] = jnp.zeros_like(st_scr)

    def scan(ci, carry):
        states = [st_scr[h] for h in heads]
        for u in range(DN_GROUP):
            c = ci * DN_GROUP + u
            rows = pl.ds(pl.multiple_of(c * c_, c_), c_)
            wq = [jnp.dot(wq_scr[c, :, lanes[h]], states[h].astype(BF16), preferred_element_type=F32)
                  for h in heads]
            v_new = [(u_scr[rows, lanes[h]] - wq[h][:c_]).astype(BF16) for h in heads]
            kv = [lax.dot_general(kd_scr[rows, lanes[h]], v_new[h], TN, preferred_element_type=F32)
                  for h in heads]
            o = [wq[h][c_:] + jnp.dot(a_scr[c, h], v_new[h], preferred_element_type=F32) for h in heads]
            states = [states[h] * eg_scr[c, h][0:1, :] + kv[h] for h in heads]
            for h in heads:
                on = o[h] * lax.rsqrt(jnp.mean(o[h] * o[h], axis=-1, keepdims=True) + EPS) * norm_ref[...]
                z = dz_ref[0, rows, lanes[h]].astype(F32)
                o_ref[0, rows, lanes[h]] = (on * (z * _sigmoid(z))).astype(BF16)
        for h in heads:
            st_scr[h] = states[h]
        return carry

    lax.fori_loop(0, n_chunks // DN_GROUP, scan, 0)


def _deltanet(pm3, ps, rows4, cw, acol, dtcol, arow, dtrow, norm, ltri, utri, shift):
    b, s, _ = pm3.shape
    n_chunks = s // CHUNK
    assert n_chunks % DN_GROUP == 0
    base = 3 * ATT_W // DN_W
    full = lambda shape: pl.BlockSpec(shape, lambda bi: (0,) * len(shape))
    return pl.pallas_call(
        functools.partial(_dn_body, n_chunks=n_chunks),
        grid=(b,),
        in_specs=[
            pl.BlockSpec((1, s, DN_W), lambda bi: (bi, 0, base)),
            pl.BlockSpec((1, s, DN_W), lambda bi: (bi, 0, base + 1)),
            pl.BlockSpec((1, s, DN_W), lambda bi: (bi, 0, base + 2)),
            pl.BlockSpec((1, s, DN_W), lambda bi: (bi, 0, base + 3)),
            pl.BlockSpec((s, D_SMALL), lambda bi: (bi, 0)),
            pl.BlockSpec((1, n_chunks, 8, CHUNK), lambda bi: (bi, 0, 0, 0)),
            full((DN_CONV, 3 * DN_W)),
            full((1, D_SMALL)), full((1, D_SMALL)), full((8, 1)), full((8, 1)),
            full((1, DN_HD)),
            full((CHUNK, 2 * CHUNK)), full((2 * CHUNK, CHUNK)), full(shift.shape),
        ],
        out_specs=pl.BlockSpec((1, s, DN_W), lambda bi: (bi, 0, 0)),
        out_shape=jax.ShapeDtypeStruct((b, s, DN_W), BF16),
        scratch_shapes=[
            pltpu.VMEM((DN_HEADS, DN_HD, DN_HD), F32),
            pltpu.VMEM((s, DN_W), F32),
            pltpu.VMEM((n_chunks, 2 * CHUNK, DN_W), BF16),
            pltpu.VMEM((s, DN_W), BF16),
            pltpu.VMEM((n_chunks, DN_HEADS, CHUNK, CHUNK), BF16),
            pltpu.VMEM((n_chunks, DN_HEADS, 8, DN_HD), F32),
        ],
        compiler_params=pltpu.CompilerParams(
            dimension_semantics=("parallel",), vmem_limit_bytes=VMEM_LIMIT),
        name="deltanet",
    )(pm3, pm3, pm3, pm3, ps, rows4, cw, acol, dtcol, arow, dtrow, norm, ltri, utri, shift)


GLA_GROUP = 2


def _gla_body(gq_ref, gk_ref, gv_ref, gz_ref, ps_ref, w2_ref, b2_ref, norm_ref, cmat_ref,
              o_ref, st_scr, att_scr, qe_scr, ke_scr, dec_scr, *, n_chunks):
    c_ = CHUNK
    rowi = lax.broadcasted_iota(jnp.int32, (c_, 1), 0)
    upper = [(rowi % (2 * s)) >= s for s in GLA_LEVELS]
    ii = lax.broadcasted_iota(jnp.int32, (2 * c_, c_), 0) % c_
    jj = lax.broadcasted_iota(jnp.int32, (2 * c_, c_), 1)
    same = [(ii // (2 * s)) == (jj // (2 * s)) for s in GLA_LEVELS]
    eye = ii == jj
    lane = lax.broadcasted_iota(jnp.int32, (c_, LANES), 1)
    first_head = lane < GLA_KD
    lane_sq = lax.broadcasted_iota(jnp.int32, (GLA_VD, LANES), 1)
    head_lanes = [lane_sq < GLA_KD, lane_sq >= GLA_KD]

    def stack_heads(t):
        return jnp.concatenate([jnp.where(first_head, t, 0.0), jnp.where(first_head, 0.0, t)], axis=0)

    pairs = range(GLA_HEADS // 2)
    pair_lanes = [slice(p * LANES, (p + 1) * LANES) for p in pairs]
    gr = GLA_GROUP * c_

    def prep(gi, carry):
        r0 = pl.multiple_of(gi * gr, gr)
        pre = (jnp.dot(ps_ref[pl.ds(r0, gr), :].astype(BF16), w2_ref[...], preferred_element_type=F32)
               + b2_ref[...])
        gk = (jnp.minimum(pre, 0.0) - jnp.log(1.0 + jnp.exp(-jnp.abs(pre)))) * (1.0 / GLA_TAU)
        g_hi, g_lo = _split_bf16(gk)
        chains = []
        for j in range(GLA_GROUP):
            rs = slice(j * c_, (j + 1) * c_)
            rows = pl.ds(r0 + j * c_, c_)
            br = jnp.dot(cmat_ref[...], jnp.concatenate([g_hi[rs], g_lo[rs]], axis=0),
                         preferred_element_type=F32)
            for p in pairs:
                q = gq_ref[0, rows, pair_lanes[p]].astype(F32) * (GLA_KD ** -0.5)
                k = gk_ref[0, rows, pair_lanes[p]].astype(F32)
                chains.append(dict(cidx=gi * GLA_GROUP + j, p=p, rows=rows, br=br, q=q, k=k,
                                   bcum=br[0:c_, pair_lanes[p]]))
        for ch in chains:
            ch["part"] = lax.dot_general(stack_heads(ch["q"]).astype(BF16), ch["k"].astype(BF16), NT,
                                         preferred_element_type=F32)
        for ch in chains:
            ch["att"] = jnp.where(eye, ch["part"], 0.0)
        for li in range(len(GLA_LEVELS)):
            up = upper[li]
            for ch in chains:
                ref = ch["br"][c_ * (li + 1):c_ * (li + 2), pair_lanes[ch["p"]]]
                e = jnp.exp(jnp.where(up, ch["bcum"] - ref, ref - ch["bcum"]))
                qt = jnp.where(up, ch["q"] * e, 0.0)
                kt = jnp.where(up, 0.0, ch["k"] * e)
                ch["part"] = lax.dot_general(stack_heads(qt).astype(BF16), kt.astype(BF16), NT,
                                             preferred_element_type=F32)
            for ch in chains:
                ch["att"] = ch["att"] + jnp.where(same[li], ch["part"], 0.0)
        for ch in chains:
            cidx, p, rows, bcum = ch["cidx"], ch["p"], ch["rows"], ch["bcum"]
            blast = bcum[c_ - 1:c_, :]
            att_scr[cidx, 2 * p] = ch["att"][:c_].astype(BF16)
            att_scr[cidx, 2 * p + 1] = ch["att"][c_:].astype(BF16)
            qe_scr[rows, pair_lanes[p]] = (ch["q"] * jnp.exp(bcum)).astype(BF16)
            ke_scr[rows, pair_lanes[p]] = (ch["k"] * jnp.exp(blast - bcum)).astype(BF16)
            dec_scr[cidx, p] = jnp.broadcast_to(jnp.exp(blast), (8, LANES))
        return carry

    lax.fori_loop(0, n_chunks // GLA_GROUP, prep, 0)

    st_scr[...] = jnp.zeros_like(st_scr)
    heads = range(GLA_HEADS)
    head_cols = [slice(h * GLA_VD, (h + 1) * GLA_VD) for h in heads]

    def scan(ci, carry):
        states = [st_scr[h] for h in heads]
        for u in range(GLA_GROUP):
            c = ci * GLA_GROUP + u
            rows = pl.ds(pl.multiple_of(c * c_, c_), c_)
            qe = [qe_scr[rows, pair_lanes[p]] for p in pairs]
            ke = [ke_scr[rows, pair_lanes[p]] for p in pairs]
            vh = [gv_ref[0, rows, head_cols[h]] for h in heads]
            upd = [lax.dot_general(vh[h], ke[h // 2], TN, preferred_element_type=F32) for h in heads]
            intra = [jnp.dot(att_scr[c, h], vh[h], preferred_element_type=F32) for h in heads]
            inter = [lax.dot_general(qe[h // 2], states[h].astype(BF16), NT, preferred_element_type=F32)
                     for h in heads]
            states = [states[h] * dec_scr[c, h // 2][0:1, :] + jnp.where(head_lanes[h % 2], upd[h], 0.0)
                      for h in heads]
            for h in heads:
                o = inter[h] + intra[h]
                o = o * lax.rsqrt(jnp.mean(o * o, axis=-1, keepdims=True) + EPS) * norm_ref[...]
                z = gz_ref[0, rows, head_cols[h]].astype(F32)
                o_ref[0, rows, head_cols[h]] = (o * (z * _sigmoid(z))).astype(BF16)
        for h in heads:
            st_scr[h] = states[h]
        return carry

    lax.fori_loop(0, n_chunks // GLA_GROUP, scan, 0)


def _gla(pm3, ps, w2, b2, norm, cmat):
    b, s, _ = pm3.shape
    n_chunks = s // CHUNK
    off = 3 * ATT_W + 4 * DN_W
    full = lambda shape: pl.BlockSpec(shape, lambda bi: (0,) * len(shape))
    return pl.pallas_call(
        functools.partial(_gla_body, n_chunks=n_chunks),
        grid=(b,),
        in_specs=[
            pl.BlockSpec((1, s, GLA_KW), lambda bi: (bi, 0, off // GLA_KW)),
            pl.BlockSpec((1, s, GLA_KW), lambda bi: (bi, 0, off // GLA_KW + 1)),
            pl.BlockSpec((1, s, GLA_W), lambda bi: (bi, 0, (off + 2 * GLA_KW) // GLA_W)),
            pl.BlockSpec((1, s, GLA_W), lambda bi: (bi, 0, (off + 2 * GLA_KW) // GLA_W + 1)),
            pl.BlockSpec((s, D_SMALL), lambda bi: (bi, 0)),
            full((D_SMALL, GLA_KW)), full((1, GLA_KW)), full((1, GLA_VD)),
            full(((len(GLA_LEVELS) + 1) * CHUNK, 2 * CHUNK)),
        ],
        out_specs=pl.BlockSpec((1, s, GLA_W), lambda bi: (bi, 0, 0)),
        out_shape=jax.ShapeDtypeStruct((b, s, GLA_W), BF16),
        scratch_shapes=[
            pltpu.VMEM((GLA_HEADS, GLA_VD, LANES), F32),
            pltpu.VMEM((n_chunks, GLA_HEADS, CHUNK, CHUNK), BF16),
            pltpu.VMEM((s, GLA_KW), BF16),
            pltpu.VMEM((s, GLA_KW), BF16),
            pltpu.VMEM((n_chunks, GLA_HEADS // 2, 8, LANES), F32),
        ],
        compiler_params=pltpu.CompilerParams(
            dimension_semantics=("parallel",), vmem_limit_bytes=VMEM_LIMIT),
        name="gla",
    )(pm3, pm3, pm3, pm3, ps, w2, b2, norm, cmat)


def _out_body(x_ref, ya_ref, yd_ref, yg_ref, p_ref, wo_ref, postg_ref, pg_ref, pp_ref, pn_ref, o_ref):
    y = (jnp.dot(ya_ref[...], wo_ref[0:ATT_W, :], preferred_element_type=F32)
         + jnp.dot(yd_ref[...], wo_ref[ATT_W:ATT_W + DN_W, :], preferred_element_type=F32)
         + jnp.dot(yg_ref[...], wo_ref[ATT_W + DN_W:D_MIX, :], preferred_element_type=F32))
    x1 = x_ref[...] + y * lax.rsqrt(jnp.mean(y * y, axis=-1, keepdims=True) + EPS) * postg_ref[...]
    gate = _sigmoid(jnp.dot(x1.astype(BF16), pg_ref[...], preferred_element_type=F32))
    pe = jnp.dot(p_ref[...].astype(BF16), pp_ref[...], preferred_element_type=F32) * gate
    o_ref[...] = x1 + pe * lax.rsqrt(jnp.mean(pe * pe, axis=-1, keepdims=True) + EPS) * pn_ref[...]


def _out(x2, ya, yd, yg, p2, wo, postg, pg, pp, pn, *, tm):
    t = x2.shape[0]
    tile = lambda w: pl.BlockSpec((tm, w), lambda i: (i, 0))
    full = lambda shape: pl.BlockSpec(shape, lambda i: (0, 0))
    return pl.pallas_call(
        _out_body,
        grid=(t // tm,),
        in_specs=[tile(D_MODEL), tile(ATT_W), tile(DN_W), tile(GLA_W), tile(D_PLE),
                  full((D_MIX, D_MODEL)), full((1, D_MODEL)), full((D_MODEL, D_MODEL)),
                  full((D_PLE, D_MODEL)), full((1, D_MODEL))],
        out_specs=tile(D_MODEL),
        out_shape=jax.ShapeDtypeStruct((t, D_MODEL), F32),
        compiler_params=pltpu.CompilerParams(
            dimension_semantics=("parallel",), vmem_limit_bytes=VMEM_LIMIT),
        name="out",
    )(x2, ya, yd, yg, p2, wo, postg, pg, pp, pn)


def _chunk_constants():
    c = CHUNK
    ltri = np.tril(np.ones((c, c), np.float32))
    blocks = [ltri]
    idx = np.arange(c)
    for s in GLA_LEVELS:
        bnd = (idx // (2 * s)) * (2 * s) + s - 1
        blocks.append(ltri[bnd])
    cmat = np.concatenate(blocks, axis=0)
    gr = DN_GROUP * c
    shift = np.zeros((DN_CONV - 1, gr, 2 * gr), np.float32)
    t = np.arange(gr)
    for i in range(DN_CONV - 1):
        shift[i, t, gr + t - (DN_CONV - 1) + i] = 1.0
    return (jnp.asarray(np.concatenate([ltri, ltri], axis=1), BF16),
            jnp.asarray(np.concatenate([ltri.T, ltri.T], axis=0), BF16),
            jnp.asarray(np.concatenate([cmat, cmat], axis=1), BF16),
            jnp.asarray(shift.reshape((DN_CONV - 1) * gr, 2 * gr), BF16))


def _pick(n, pref):
    for t in pref:
        if n % t == 0:
            return t
    return n


def _layer(x, p_l, w_in, w_out, pre_g, post_g, lq1, lk1, lq2, lk2, att_subln, dn_conv, dn_a_log,
           dn_dt_bias, dn_norm, gla_w2, gla_b, gla_norm, ple_proj, ple_gate, ple_norm, layer_idx, consts):
    b, s, _ = x.shape
    t = b * s
    ltri, utri, cmat, shift = consts
    lam_init = 0.8 - 0.6 * math.exp(-0.3 * layer_idx)

    n_big = 4 * ATT_W + 4 * DN_W
    n_small = 2 * DN_HEADS
    wm = jnp.concatenate([w_in[:, :2 * ATT_W], w_in[:, 3 * ATT_W:n_big],
                          w_in[:, n_big + n_small:n_big + n_small + 2 * GLA_KW + 2 * GLA_W]],
                         axis=1).astype(BF16)
    wvt = w_in[:, 2 * ATT_W:3 * ATT_W].T.astype(BF16)
    w_small = jnp.concatenate([w_in[:, n_big:n_big + n_small], w_in[:, -GLA_RANK:]], axis=1)
    ws = jnp.pad(w_small, ((0, 0), (0, D_SMALL - w_small.shape[1]))).astype(BF16)
    wst = w_small[:, :n_small].T.astype(BF16)

    x2 = x.reshape(t, D_MODEL)
    pm, ps, pst, vt = _proj(x2, pre_g.reshape(1, D_MODEL), wm, ws, wst, wvt, b=b, s=s,
                            tm=_pick(s, (1024, 512, 256)), tn=_pick(D_MAIN, (1664, 512)))
    pm3 = pm.reshape(b, s, D_MAIN)

    lqk = jnp.stack([lq1, lk1, lq2, lk2], axis=0)
    y_att = _attn(pm3, vt, lqk, att_subln.reshape(1, ATT_VD), tq=LANES, lam_init=lam_init)

    rows4 = pst.reshape(8, b, s // CHUNK, CHUNK).transpose(1, 2, 0, 3)
    lane_vec = lambda v, off: jnp.zeros((1, D_SMALL), F32).at[0, off:off + v.shape[0]].set(v)
    row_vec = lambda v, off: jnp.zeros((8, 1), F32).at[off:off + v.shape[0], 0].set(v)
    y_dn = _deltanet(pm3, ps, rows4, dn_conv, lane_vec(dn_a_log, DN_HEADS), lane_vec(dn_dt_bias, DN_HEADS),
                     row_vec(dn_a_log, DN_HEADS), row_vec(dn_dt_bias, DN_HEADS),
                     dn_norm.reshape(1, DN_HD), ltri, utri, shift)

    w2 = jnp.zeros((D_SMALL, GLA_KW), F32).at[n_small:n_small + GLA_RANK].set(gla_w2).astype(BF16)
    y_gla = _gla(pm3, ps, w2, gla_b.reshape(1, GLA_KW), gla_norm.reshape(1, GLA_VD), cmat)

    out = _out(x2, y_att.reshape(t, ATT_W), y_dn.reshape(t, DN_W), y_gla.reshape(t, GLA_W),
               p_l.reshape(t, D_PLE), w_out.astype(BF16), post_g.reshape(1, D_MODEL),
               ple_gate.astype(BF16), ple_proj.astype(BF16), ple_norm.reshape(1, D_MODEL),
               tm=_pick(t, (512, 256)))
    return out.reshape(b, s, D_MODEL)


def kernel(x, p, w_in, w_out, pre_gain, post_gain, att_lq1, att_lk1, att_lq2, att_lk2, att_subln, dn_conv, dn_a_log, dn_dt_bias, dn_norm, gla_w2, gla_b, gla_norm, ple_proj, ple_gate, ple_norm):
    consts = _chunk_constants()
    for i in range(p.shape[0]):
        x = _layer(x, p[i], w_in[i], w_out[i], pre_gain[i], post_gain[i], att_lq1[i], att_lk1[i],
                   att_lq2[i], att_lk2[i], att_subln[i], dn_conv[i], dn_a_log[i], dn_dt_bias[i],
                   dn_norm[i], gla_w2[i], gla_b[i], gla_norm[i], ple_proj[i], ple_gate[i],
                   ple_norm[i], i, consts)
    return x
```

```python
import functools
import math

import numpy as np
import jax
import jax.numpy as jnp
from jax import lax
from jax.experimental import pallas as pl
from jax.experimental.pallas import tpu as pltpu

F32 = jnp.float32
BF16 = jnp.bfloat16

D_MODEL = 1024
D_PLE = 256
EPS = 1e-6
ATT_HEADS = 8
ATT_HD = 64
ATT_VD = 128
ATT_W = 1024
DN_HEADS = 4
DN_HD = 128
DN_W = 512
DN_CONV = 4
GLA_HEADS = 4
GLA_KD = 64
GLA_VD = 128
GLA_KW = 256
GLA_W = 512
GLA_RANK = 16
GLA_TAU = 16.0
CHUNK = 64
D_MIX = 2048
D_MAIN = 6656
D_SMALL = 128
LANES = 128
VMEM_LIMIT = 48 * 1024 * 1024

NT = (((1,), (1,)), ((), ()))
TN = (((0,), (0,)), ((), ()))
GLA_LEVELS = (32, 16, 8, 4, 2, 1)
LOG2E = math.log2(math.e)
ATT_ONES = 16


def _sigmoid(x):
    return 1.0 / (1.0 + jnp.exp(-x))


def _softplus(x):
    return jnp.maximum(x, 0.0) + jnp.log(1.0 + jnp.exp(-jnp.abs(x)))


def _split_bf16(x):
    hi = x.astype(BF16)
    lo = (x - hi.astype(F32)).astype(BF16)
    return hi, lo


def _proj_body(x_ref, g_ref, wm_ref, ws_ref, wst_ref, wvt_ref, pm_ref, ps_ref, pst_ref, vt_ref, h_scr):
    @pl.when(pl.program_id(1) == 0)
    def _():
        x = x_ref[...]
        h = (x * lax.rsqrt(jnp.mean(x * x, axis=-1, keepdims=True) + EPS) * g_ref[...]).astype(BF16)
        h_scr[...] = h
        ps_ref[...] = jnp.dot(h, ws_ref[...], preferred_element_type=F32)
        pst_ref[...] = lax.dot_general(wst_ref[...], h, NT, preferred_element_type=F32)
        vt_ref[0] = lax.dot_general(wvt_ref[...], h, NT, preferred_element_type=F32).astype(BF16)

    pm_ref[...] = jnp.dot(h_scr[...], wm_ref[...], preferred_element_type=F32).astype(BF16)


def _proj(x2, g, wm, ws, wst, wvt, *, b, s, tm, tn):
    t = x2.shape[0]
    spb = s // tm
    return pl.pallas_call(
        _proj_body,
        grid=(t // tm, D_MAIN // tn),
        in_specs=[
            pl.BlockSpec((tm, D_MODEL), lambda i, j: (i, 0)),
            pl.BlockSpec((1, D_MODEL), lambda i, j: (0, 0)),
            pl.BlockSpec((D_MODEL, tn), lambda i, j: (0, j)),
            pl.BlockSpec((D_MODEL, D_SMALL), lambda i, j: (0, 0)),
            pl.BlockSpec((8, D_MODEL), lambda i, j: (0, 0)),
            pl.BlockSpec((ATT_W, D_MODEL), lambda i, j: (0, 0)),
        ],
        out_specs=[
            pl.BlockSpec((tm, tn), lambda i, j: (i, j)),
            pl.BlockSpec((tm, D_SMALL), lambda i, j: (i, 0)),
            pl.BlockSpec((8, tm), lambda i, j: (0, i)),
            pl.BlockSpec((1, ATT_W, tm), lambda i, j: (i // spb, 0, i % spb)),
        ],
        out_shape=[
            jax.ShapeDtypeStruct((t, D_MAIN), BF16),
            jax.ShapeDtypeStruct((t, D_SMALL), F32),
            jax.ShapeDtypeStruct((8, t), F32),
            jax.ShapeDtypeStruct((b, ATT_W, s), BF16),
        ],
        scratch_shapes=[pltpu.VMEM((tm, D_MODEL), BF16)],
        compiler_params=pltpu.CompilerParams(
            dimension_semantics=("parallel", "arbitrary"), vmem_limit_bytes=VMEM_LIMIT),
        name="proj",
    )(x2, g, wm, ws, wst, wvt)


def _attn_body(q_ref, k_ref, vt_ref, z_ref, lqk_ref, subln_ref, o_ref, acc_scr, *, tq, tk, n_q, lam_init):
    h = pl.program_id(1)
    sub = LANES
    nsub = tk // sub
    nch = tk // tq
    nr = 2 * tq
    lane = lax.broadcasted_iota(jnp.int32, (tq, LANES), 1)
    scale = ATT_HD ** -0.5 * LOG2E
    slope_row = LOG2E * jnp.exp2(jnp.full((1, nr), -8.0 / ATT_HEADS, F32) * (h + 1).astype(F32))
    krow = lax.broadcasted_iota(jnp.int32, (sub, nr), 0)
    qcol = lax.broadcasted_iota(jnp.int32, (sub, nr), 1)
    lag = krow - jnp.where(qcol >= tq, qcol - tq, qcol)
    bias = slope_row * krow.astype(F32)
    ones_rows = jnp.ones((ATT_ONES, tk), BF16)
    lqk = lqk_ref[...]
    lam = (jnp.exp(jnp.sum(lqk[0:1] * lqk[1:2], axis=-1, keepdims=True))
           - jnp.exp(jnp.sum(lqk[2:3] * lqk[3:4], axis=-1, keepdims=True)) + lam_init)

    chains = range(nch)

    def load_queries(qb):
        qs = []
        for c in chains:
            q = q_ref[0, qb * tk + c * tq:qb * tk + (c + 1) * tq, :].astype(F32) * scale
            qs.append(jnp.concatenate([jnp.where(lane < ATT_HD, q, 0.0),
                                       jnp.where(lane >= ATT_HD, q, 0.0)], axis=0).astype(BF16))
        return qs

    def key_count(diag):
        return [c + 1 if diag else nsub for c in chains]

    def scores(qs, j, diag):
        ns = key_count(diag)
        return [lax.dot_general(k_ref[0, j * tk:j * tk + ns[c] * sub, :], qs[c], NT,
                                preferred_element_type=F32) for c in chains]

    def update(qb, j, s_all, ms, diag):
        ns = key_count(diag)
        vt = jnp.concatenate([vt_ref[0, :, j * tk:(j + 1) * tk], ones_rows], axis=0)
        s, shift, ms_new = [], [], []
        for c in chains:
            sc, shc = [], []
            m_new = ms[c]
            for t in range(ns[c]):
                st = s_all[c][t * sub:(t + 1) * sub] + bias
                if diag and t == c:
                    st = jnp.where(lag <= 0, st, -1e30)
                sh = slope_row * float((j - qb) * tk + (t - c) * sub)
                m_new = jnp.maximum(m_new, jnp.max(st, axis=0, keepdims=True) + sh)
                sc.append(st)
                shc.append(sh)
            s.append(sc)
            shift.append(shc)
            ms_new.append(m_new)
        p = [jnp.concatenate([jnp.exp2((s[c][t] - (ms_new[c] - shift[c][t])).astype(BF16))
                              for t in range(ns[c])], axis=0) for c in chains]
        pv = [jnp.dot(vt[:, :ns[c] * sub], p[c], preferred_element_type=F32) for c in chains]
        for c in chains:
            if j == 0:
                acc_scr[c] = pv[c]
            else:
                acc_scr[c] = jnp.exp2(ms[c] - ms_new[c]) * acc_scr[c] + pv[c]
        return ms_new

    def finish(qb):
        for c in chains:
            rows = slice(qb * tk + c * tq, qb * tk + (c + 1) * tq)
            acc = acc_scr[c]
            ot = acc[:ATT_VD] * (1.0 / acc[ATT_VD:ATT_VD + 1])
            o = (ot[:, :tq] - lam * ot[:, tq:]).T
            o = o * lax.rsqrt(jnp.mean(o * o, axis=-1, keepdims=True) + EPS) * subln_ref[...] * (1.0 - lam_init)
            z = z_ref[0, rows, :].astype(F32)
            o_ref[0, rows, :] = (o * (z * _sigmoid(z))).astype(BF16)

    steps = [(qb, j) for qb in range(n_q // nch) for j in range(qb + 1)]
    qs = load_queries(0)
    s_next = scores(qs, 0, True)
    ms = None
    for n, (qb, j) in enumerate(steps):
        s_cur = s_next
        if n + 1 < len(steps):
            qb2, j2 = steps[n + 1]
            if qb2 != qb:
                qs = load_queries(qb2)
            s_next = scores(qs, j2, j2 == qb2)
        if j == 0:
            ms = [jnp.full((1, nr), -1e30, F32) for _ in chains]
        ms = update(qb, j, s_cur, ms, j == qb)
        if j == qb:
            finish(qb)


def _attn(pm3, vt, lqk, subln, *, tq, lam_init):
    b, s, _ = pm3.shape
    nq = ATT_W // LANES
    seq = lambda col: pl.BlockSpec((1, s, LANES), lambda bi, h: (bi, 0, col + h))
    tk = _pick(s, (512, 256, 128))
    return pl.pallas_call(
        functools.partial(_attn_body, tq=tq, tk=tk, n_q=s // tq, lam_init=lam_init),
        grid=(b, ATT_HEADS),
        in_specs=[
            seq(0), seq(nq),
            pl.BlockSpec((1, LANES, s), lambda bi, h: (bi, h, 0)),
            seq(2 * nq),
            pl.BlockSpec((4, ATT_HD), lambda bi, h: (0, 0)),
            pl.BlockSpec((1, ATT_VD), lambda bi, h: (0, 0)),
        ],
        out_specs=pl.BlockSpec((1, s, LANES), lambda bi, h: (bi, 0, h)),
        out_shape=jax.ShapeDtypeStruct((b, s, ATT_W), BF16),
        scratch_shapes=[pltpu.VMEM((tk // tq, ATT_VD + ATT_ONES, 2 * tq), F32)],
        compiler_params=pltpu.CompilerParams(
            dimension_semantics=("parallel", "parallel"), vmem_limit_bytes=VMEM_LIMIT),
        name="attn",
    )(pm3, pm3, vt, pm3, lqk, subln)


DN_GROUP = 2


def _dn_body(dq_ref, dk_ref, dv_ref, dz_ref, ps_ref, rows_ref, cw_ref, acol_ref, dtcol_ref,
             arow_ref, dtrow_ref, norm_ref, ltri_ref, utri_ref, shift_ref, o_ref,
             st_scr, u_scr, wq_scr, kd_scr, a_scr, eg_scr, *, n_chunks):
    c_ = CHUNK
    gr = DN_GROUP * c_
    ii = lax.broadcasted_iota(jnp.int32, (c_, c_), 0)
    jj = lax.broadcasted_iota(jnp.int32, (c_, c_), 1)
    tri_incl = ii >= jj
    tri_strict = ii > jj
    eye = ii == jj
    cw = cw_ref[...]
    heads = range(DN_HEADS)
    lanes = [slice(h * DN_HD, (h + 1) * DN_HD) for h in heads]

    def conv_silu(win, w4):
        delayed = jnp.dot(shift_ref[...], win, preferred_element_type=F32)
        y = (delayed[0:gr] * w4[0:1] + delayed[gr:2 * gr] * w4[1:2] + delayed[2 * gr:3 * gr] * w4[2:3]
             + win[gr:2 * gr].astype(F32) * w4[3:4])
        return y * _sigmoid(y)

    def window(ref, r0, first):
        if first:
            return jnp.concatenate([jnp.zeros((gr, DN_W), BF16), ref[0, 0:gr, :]], axis=0)
        return ref[0, pl.ds(pl.multiple_of(r0 - gr, gr), 2 * gr), :]

    def mm(a, b):
        return jnp.dot(a.astype(BF16), b.astype(BF16), preferred_element_type=F32)

    def prep(gi, first):
        r0 = 0 if first else pl.multiple_of(gi * gr, gr)
        cq = conv_silu(window(dq_ref, r0, first), cw[:, 0:DN_W])
        ck = conv_silu(window(dk_ref, r0, first), cw[:, DN_W:2 * DN_W])
        cv = conv_silu(window(dv_ref, r0, first), cw[:, 2 * DN_W:3 * DN_W])
        ps = ps_ref[pl.ds(r0, gr), :]
        beta_all = _sigmoid(ps)
        g_all = -jnp.exp(acol_ref[...]) * _softplus(ps + dtcol_ref[...])
        g_hi, g_lo = _split_bf16(g_all)
        chains = []
        for j in range(DN_GROUP):
            cidx = gi * DN_GROUP + j
            rs = slice(j * c_, (j + 1) * c_)
            gc_all = jnp.dot(ltri_ref[...], jnp.concatenate([g_hi[rs], g_lo[rs]], axis=0),
                             preferred_element_type=F32)
            g_row = -jnp.exp(arow_ref[...]) * _softplus(rows_ref[0, cidx] + dtrow_ref[...])
            r_hi, r_lo = _split_bf16(g_row)
            gc_rows = jnp.dot(jnp.concatenate([r_hi, r_lo], axis=1), utri_ref[...],
                              preferred_element_type=F32)
            for h in heads:
                qh, kh, vh = cq[rs, lanes[h]], ck[rs, lanes[h]], cv[rs, lanes[h]]
                qn = qh * (lax.rsqrt(jnp.sum(qh * qh, axis=-1, keepdims=True) + EPS) * DN_HD ** -0.5)
                kn = kh * lax.rsqrt(jnp.sum(kh * kh, axis=-1, keepdims=True) + EPS)
                beta = beta_all[rs, h:h + 1]
                gcc = gc_all[:, 4 + h:5 + h]
                gcr = gc_rows[4 + h:5 + h, :]
                gl = gcr[:, c_ - 1:c_]
                chains.append(dict(
                    cidx=cidx, h=h, rows=pl.ds(r0 + j * c_, c_), qn=qn, kn=kn, kb=kn * beta, vb=vh * beta,
                    gcc=gcc, gl=gl, decay=jnp.exp(jnp.where(tri_incl, gcc - gcr, -1e30))))
        for ch in chains:
            ch["gram"] = lax.dot_general(jnp.concatenate([ch["kb"], ch["qn"]], axis=0).astype(BF16),
                                         ch["kn"].astype(BF16), NT, preferred_element_type=F32)
        for ch in chains:
            x = jnp.where(tri_strict, -(ch["gram"][:c_] * ch["decay"]), 0.0)
            a_scr[ch["cidx"], ch["h"]] = (ch["gram"][c_:] * ch["decay"]).astype(BF16)
            ch["acc"] = jnp.where(eye, 1.0, x)
            ch["x"] = x
        for ch in chains:
            ch["xp"] = mm(ch["x"], ch["x"])
        for _ in range(4):
            for ch in chains:
                ch["both"] = mm(jnp.concatenate([ch["acc"], ch["xp"]], axis=0), ch["xp"])
            for ch in chains:
                ch["acc"] = ch["acc"] + ch["both"][:c_]
                ch["xp"] = ch["both"][c_:]
        for ch in chains:
            ch["last"] = mm(ch["acc"], ch["xp"])
        for ch in chains:
            t_inv = ch["acc"] + ch["last"]
            gexp = jnp.exp(ch["gcc"])
            ch["gexp"] = gexp
            ch["uw"] = mm(t_inv, jnp.concatenate([ch["vb"], ch["kb"] * gexp], axis=1))
        for ch in chains:
            cidx, h, rows = ch["cidx"], ch["h"], ch["rows"]
            u_scr[rows, lanes[h]] = ch["uw"][:, :DN_HD]
            wq_scr[cidx, 0:c_, lanes[h]] = ch["uw"][:, DN_HD:].astype(BF16)
            wq_scr[cidx, c_:2 * c_, lanes[h]] = (ch["qn"] * ch["gexp"]).astype(BF16)
            kd_scr[rows, lanes[h]] = (ch["kn"] * jnp.exp(ch["gl"] - ch["gcc"])).astype(BF16)
            eg_scr[cidx, h] = jnp.broadcast_to(jnp.exp(ch["gl"]), (8, DN_HD))

    prep(0, True)

    def prep_body(gi, carry):
        prep(gi, False)
        return carry

    lax.fori_loop(1, n_chunks // DN_GROUP, prep_body, 0)

    st_scr[...] = jnp.zeros_like(st_scr)

    def scan(ci, carry):
        states = [st_scr[h] for h in heads]
        for u in range(DN_GROUP):
            c = ci * DN_GROUP + u
            rows = pl.ds(pl.multiple_of(c * c_, c_), c_)
            wq = [jnp.dot(wq_scr[c, :, lanes[h]], states[h].astype(BF16), preferred_element_type=F32)
                  for h in heads]
            v_new = [(u_scr[rows, lanes[h]] - wq[h][:c_]).astype(BF16) for h in heads]
            kv = [lax.dot_general(kd_scr[rows, lanes[h]], v_new[h], TN, preferred_element_type=F32)
                  for h in heads]
            o = [wq[h][c_:] + jnp.dot(a_scr[c, h], v_new[h], preferred_element_type=F32) for h in heads]
            states = [states[h] * eg_scr[c, h][0:1, :] + kv[h] for h in heads]
            for h in heads:
                on = o[h] * lax.rsqrt(jnp.mean(o[h] * o[h], axis=-1, keepdims=True) + EPS) * norm_ref[...]
                z = dz_ref[0, rows, lanes[h]].astype(F32)
                o_ref[0, rows, lanes[h]] = (on * (z * _sigmoid(z))).astype(BF16)
        for h in heads:
            st_scr[h] = states[h]
        return carry

    lax.fori_loop(0, n_chunks // DN_GROUP, scan, 0)


def _deltanet(pm3, ps, rows4, cw, acol, dtcol, arow, dtrow, norm, ltri, utri, shift):
    b, s, _ = pm3.shape
    n_chunks = s // CHUNK
    assert n_chunks % DN_GROUP == 0
    base = 3 * ATT_W // DN_W
    full = lambda shape: pl.BlockSpec(shape, lambda bi: (0,) * len(shape))
    return pl.pallas_call(
        functools.partial(_dn_body, n_chunks=n_chunks),
        grid=(b,),
        in_specs=[
            pl.BlockSpec((1, s, DN_W), lambda bi: (bi, 0, base)),
            pl.BlockSpec((1, s, DN_W), lambda bi: (bi, 0, base + 1)),
            pl.BlockSpec((1, s, DN_W), lambda bi: (bi, 0, base + 2)),
            pl.BlockSpec((1, s, DN_W), lambda bi: (bi, 0, base + 3)),
            pl.BlockSpec((s, D_SMALL), lambda bi: (bi, 0)),
            pl.BlockSpec((1, n_chunks, 8, CHUNK), lambda bi: (bi, 0, 0, 0)),
            full((DN_CONV, 3 * DN_W)),
            full((1, D_SMALL)), full((1, D_SMALL)), full((8, 1)), full((8, 1)),
            full((1, DN_HD)),
            full((CHUNK, 2 * CHUNK)), full((2 * CHUNK, CHUNK)), full(shift.shape),
        ],
        out_specs=pl.BlockSpec((1, s, DN_W), lambda bi: (bi, 0, 0)),
        out_shape=jax.ShapeDtypeStruct((b, s, DN_W), BF16),
        scratch_shapes=[
            pltpu.VMEM((DN_HEADS, DN_HD, DN_HD), F32),
            pltpu.VMEM((s, DN_W), F32),
            pltpu.VMEM((n_chunks, 2 * CHUNK, DN_W), BF16),
            pltpu.VMEM((s, DN_W), BF16),
            pltpu.VMEM((n_chunks, DN_HEADS, CHUNK, CHUNK), BF16),
            pltpu.VMEM((n_chunks, DN_HEADS, 8, DN_HD), F32),
        ],
        compiler_params=pltpu.CompilerParams(
            dimension_semantics=("parallel",), vmem_limit_bytes=VMEM_LIMIT),
        name="deltanet",
    )(pm3, pm3, pm3, pm3, ps, rows4, cw, acol, dtcol, arow, dtrow, norm, ltri, utri, shift)


GLA_GROUP = 2


def _gla_body(gq_ref, gk_ref, gv_ref, gz_ref, ps_ref, w2_ref, b2_ref, norm_ref, cmat_ref,
              o_ref, st_scr, att_scr, qe_scr, ke_scr, dec_scr, *, n_chunks):
    c_ = CHUNK
    rowi = lax.broadcasted_iota(jnp.int32, (c_, 1), 0)
    upper = [(rowi % (2 * s)) >= s for s in GLA_LEVELS]
    ii = lax.broadcasted_iota(jnp.int32, (2 * c_, c_), 0) % c_
    jj = lax.broadcasted_iota(jnp.int32, (2 * c_, c_), 1)
    same = [(ii // (2 * s)) == (jj // (2 * s)) for s in GLA_LEVELS]
    eye = ii == jj
    lane = lax.broadcasted_iota(jnp.int32, (c_, LANES), 1)
    first_head = lane < GLA_KD
    lane_sq = lax.broadcasted_iota(jnp.int32, (GLA_VD, LANES), 1)
    head_lanes = [lane_sq < GLA_KD, lane_sq >= GLA_KD]

    def stack_heads(t):
        return jnp.concatenate([jnp.where(first_head, t, 0.0), jnp.where(first_head, 0.0, t)], axis=0)

    pairs = range(GLA_HEADS // 2)
    pair_lanes = [slice(p * LANES, (p + 1) * LANES) for p in pairs]
    gr = GLA_GROUP * c_

    def prep(gi, carry):
        r0 = pl.multiple_of(gi * gr, gr)
        pre = (jnp.dot(ps_ref[pl.ds(r0, gr), :].astype(BF16), w2_ref[...], preferred_element_type=F32)
               + b2_ref[...])
        gk = (jnp.minimum(pre, 0.0) - jnp.log(1.0 + jnp.exp(-jnp.abs(pre)))) * (1.0 / GLA_TAU)
        g_hi, g_lo = _split_bf16(gk)
        chains = []
        for j in range(GLA_GROUP):
            rs = slice(j * c_, (j + 1) * c_)
            rows = pl.ds(r0 + j * c_, c_)
            br = jnp.dot(cmat_ref[...], jnp.concatenate([g_hi[rs], g_lo[rs]], axis=0),
                         preferred_element_type=F32)
            for p in pairs:
                q = gq_ref[0, rows, pair_lanes[p]].astype(F32) * (GLA_KD ** -0.5)
                k = gk_ref[0, rows, pair_lanes[p]].astype(F32)
                chains.append(dict(cidx=gi * GLA_GROUP + j, p=p, rows=rows, br=br, q=q, k=k,
                                   bcum=br[0:c_, pair_lanes[p]]))
        for ch in chains:
            ch["part"] = lax.dot_general(stack_heads(ch["q"]).astype(BF16), ch["k"].astype(BF16), NT,
                                         preferred_element_type=F32)
        for ch in chains:
            ch["att"] = jnp.where(eye, ch["part"], 0.0)
        for li in range(len(GLA_LEVELS)):
            up = upper[li]
            for ch in chains:
                ref = ch["br"][c_ * (li + 1):c_ * (li + 2), pair_lanes[ch["p"]]]
                e = jnp.exp(jnp.where(up, ch["bcum"] - ref, ref - ch["bcum"]))
                qt = jnp.where(up, ch["q"] * e, 0.0)
                kt = jnp.where(up, 0.0, ch["k"] * e)
                ch["part"] = lax.dot_general(stack_heads(qt).astype(BF16), kt.astype(BF16), NT,
                                             preferred_element_type=F32)
            for ch in chains:
                ch["att"] = ch["att"] + jnp.where(same[li], ch["part"], 0.0)
        for ch in chains:
            cidx, p, rows, bcum = ch["cidx"], ch["p"], ch["rows"], ch["bcum"]
            blast = bcum[c_ - 1:c_, :]
            att_scr[cidx, 2 * p] = ch["att"][:c_].astype(BF16)
            att_scr[cidx, 2 * p + 1] = ch["att"][c_:].astype(BF16)
            qe_scr[rows, pair_lanes[p]] = (ch["q"] * jnp.exp(bcum)).astype(BF16)
            ke_scr[rows, pair_lanes[p]] = (ch["k"] * jnp.exp(blast - bcum)).astype(BF16)
            dec_scr[cidx, p] = jnp.broadcast_to(jnp.exp(blast), (8, LANES))
        return carry

    lax.fori_loop(0, n_chunks // GLA_GROUP, prep, 0)

    st_scr[...] = jnp.zeros_like(st_scr)
    heads = range(GLA_HEADS)
    head_cols = [slice(h * GLA_VD, (h + 1) * GLA_VD) for h in heads]

    def scan(ci, carry):
        states = [st_scr[h] for h in heads]
        for u in range(GLA_GROUP):
            c = ci * GLA_GROUP + u
            rows = pl.ds(pl.multiple_of(c * c_, c_), c_)
            qe = [qe_scr[rows, pair_lanes[p]] for p in pairs]
            ke = [ke_scr[rows, pair_lanes[p]] for p in pairs]
            vh = [gv_ref[0, rows, head_cols[h]] for h in heads]
            upd = [lax.dot_general(vh[h], ke[h // 2], TN, preferred_element_type=F32) for h in heads]
            intra = [jnp.dot(att_scr[c, h], vh[h], preferred_element_type=F32) for h in heads]
            inter = [lax.dot_general(qe[h // 2], states[h].astype(BF16), NT, preferred_element_type=F32)
                     for h in heads]
            states = [states[h] * dec_scr[c, h // 2][0:1, :] + jnp.where(head_lanes[h % 2], upd[h], 0.0)
                      for h in heads]
            for h in heads:
                o = inter[h] + intra[h]
                o = o * lax.rsqrt(jnp.mean(o * o, axis=-1, keepdims=True) + EPS) * norm_ref[...]
                z = gz_ref[0, rows, head_cols[h]].astype(F32)
                o_ref[0, rows, head_cols[h]] = (o * (z * _sigmoid(z))).astype(BF16)
        for h in heads:
            st_scr[h] = states[h]
        return carry

    lax.fori_loop(0, n_chunks // GLA_GROUP, scan, 0)


def _gla(pm3, ps, w2, b2, norm, cmat):
    b, s, _ = pm3.shape
    n_chunks = s // CHUNK
    off = 3 * ATT_W + 4 * DN_W
    full = lambda shape: pl.BlockSpec(shape, lambda bi: (0,) * len(shape))
    return pl.pallas_call(
        functools.partial(_gla_body, n_chunks=n_chunks),
        grid=(b,),
        in_specs=[
            pl.BlockSpec((1, s, GLA_KW), lambda bi: (bi, 0, off // GLA_KW)),
            pl.BlockSpec((1, s, GLA_KW), lambda bi: (bi, 0, off // GLA_KW + 1)),
            pl.BlockSpec((1, s, GLA_W), lambda bi: (bi, 0, (off + 2 * GLA_KW) // GLA_W)),
            pl.BlockSpec((1, s, GLA_W), lambda bi: (bi, 0, (off + 2 * GLA_KW) // GLA_W + 1)),
            pl.BlockSpec((s, D_SMALL), lambda bi: (bi, 0)),
            full((D_SMALL, GLA_KW)), full((1, GLA_KW)), full((1, GLA_VD)),
            full(((len(GLA_LEVELS) + 1) * CHUNK, 2 * CHUNK)),
        ],
        out_specs=pl.BlockSpec((1, s, GLA_W), lambda bi: (bi, 0, 0)),
        out_shape=jax.ShapeDtypeStruct((b, s, GLA_W), BF16),
        scratch_shapes=[
            pltpu.VMEM((GLA_HEADS, GLA_VD, LANES), F32),
            pltpu.VMEM((n_chunks, GLA_HEADS, CHUNK, CHUNK), BF16),
            pltpu.VMEM((s, GLA_KW), BF16),
            pltpu.VMEM((s, GLA_KW), BF16),
            pltpu.VMEM((n_chunks, GLA_HEADS // 2, 8, LANES), F32),
        ],
        compiler_params=pltpu.CompilerParams(
            dimension_semantics=("parallel",), vmem_limit_bytes=VMEM_LIMIT),
        name="gla",
    )(pm3, pm3, pm3, pm3, ps, w2, b2, norm, cmat)


def _out_body(x_ref, ya_ref, yd_ref, yg_ref, p_ref, wo_ref, postg_ref, pg_ref, pp_ref, pn_ref, o_ref):
    y = (jnp.dot(ya_ref[...], wo_ref[0:ATT_W, :], preferred_element_type=F32)
         + jnp.dot(yd_ref[...], wo_ref[ATT_W:ATT_W + DN_W, :], preferred_element_type=F32)
         + jnp.dot(yg_ref[...], wo_ref[ATT_W + DN_W:D_MIX, :], preferred_element_type=F32))
    x1 = x_ref[...] + y * lax.rsqrt(jnp.mean(y * y, axis=-1, keepdims=True) + EPS) * postg_ref[...]
    gate = _sigmoid(jnp.dot(x1.astype(BF16), pg_ref[...], preferred_element_type=F32))
    pe = jnp.dot(p_ref[...].astype(BF16), pp_ref[...], preferred_element_type=F32) * gate
    o_ref[...] = x1 + pe * lax.rsqrt(jnp.mean(pe * pe, axis=-1, keepdims=True) + EPS) * pn_ref[...]


def _out(x2, ya, yd, yg, p2, wo, postg, pg, pp, pn, *, tm):
    t = x2.shape[0]
    tile = lambda w: pl.BlockSpec((tm, w), lambda i: (i, 0))
    full = lambda shape: pl.BlockSpec(shape, lambda i: (0, 0))
    return pl.pallas_call(
        _out_body,
        grid=(t // tm,),
        in_specs=[tile(D_MODEL), tile(ATT_W), tile(DN_W), tile(GLA_W), tile(D_PLE),
                  full((D_MIX, D_MODEL)), full((1, D_MODEL)), full((D_MODEL, D_MODEL)),
                  full((D_PLE, D_MODEL)), full((1, D_MODEL))],
        out_specs=tile(D_MODEL),
        out_shape=jax.ShapeDtypeStruct((t, D_MODEL), F32),
        compiler_params=pltpu.CompilerParams(
            dimension_semantics=("parallel",), vmem_limit_bytes=VMEM_LIMIT),
        name="out",
    )(x2, ya, yd, yg, p2, wo, postg, pg, pp, pn)


def _chunk_constants():
    c = CHUNK
    ltri = np.tril(np.ones((c, c), np.float32))
    blocks = [ltri]
    idx = np.arange(c)
    for s in GLA_LEVELS:
        bnd = (idx // (2 * s)) * (2 * s) + s - 1
        blocks.append(ltri[bnd])
    cmat = np.concatenate(blocks, axis=0)
    gr = DN_GROUP * c
    shift = np.zeros((DN_CONV - 1, gr, 2 * gr), np.float32)
    t = np.arange(gr)
    for i in range(DN_CONV - 1):
        shift[i, t, gr + t - (DN_CONV - 1) + i] = 1.0
    return (jnp.asarray(np.concatenate([ltri, ltri], axis=1), BF16),
            jnp.asarray(np.concatenate([ltri.T, ltri.T], axis=0), BF16),
            jnp.asarray(np.concatenate([cmat, cmat], axis=1), BF16),
            jnp.asarray(shift.reshape((DN_CONV - 1) * gr, 2 * gr), BF16))


def _pick(n, pref):
    for t in pref:
        if n % t == 0:
            return t
    return n


def _layer(x, p_l, w_in, w_out, pre_g, post_g, lq1, lk1, lq2, lk2, att_subln, dn_conv, dn_a_log,
           dn_dt_bias, dn_norm, gla_w2, gla_b, gla_norm, ple_proj, ple_gate, ple_norm, layer_idx, consts):
    b, s, _ = x.shape
    t = b * s
    ltri, utri, cmat, shift = consts
    lam_init = 0.8 - 0.6 * math.exp(-0.3 * layer_idx)

    n_big = 4 * ATT_W + 4 * DN_W
    n_small = 2 * DN_HEADS
    wm = jnp.concatenate([w_in[:, :2 * ATT_W], w_in[:, 3 * ATT_W:n_big],
                          w_in[:, n_big + n_small:n_big + n_small + 2 * GLA_KW + 2 * GLA_W]],
                         axis=1).astype(BF16)
    wvt = w_in[:, 2 * ATT_W:3 * ATT_W].T.astype(BF16)
    w_small = jnp.concatenate([w_in[:, n_big:n_big + n_small], w_in[:, -GLA_RANK:]], axis=1)
    ws = jnp.pad(w_small, ((0, 0), (0, D_SMALL - w_small.shape[1]))).astype(BF16)
    wst = w_small[:, :n_small].T.astype(BF16)

    x2 = x.reshape(t, D_MODEL)
    pm, ps, pst, vt = _proj(x2, pre_g.reshape(1, D_MODEL), wm, ws, wst, wvt, b=b, s=s,
                            tm=_pick(s, (1024, 512, 256)), tn=_pick(D_MAIN, (1664, 512)))
    pm3 = pm.reshape(b, s, D_MAIN)

    lqk = jnp.stack([lq1, lk1, lq2, lk2], axis=0)
    y_att = _attn(pm3, vt, lqk, att_subln.reshape(1, ATT_VD), tq=LANES, lam_init=lam_init)

    rows4 = pst.reshape(8, b, s // CHUNK, CHUNK).transpose(1, 2, 0, 3)
    lane_vec = lambda v, off: jnp.zeros((1, D_SMALL), F32).at[0, off:off + v.shape[0]].set(v)
    row_vec = lambda v, off: jnp.zeros((8, 1), F32).at[off:off + v.shape[0], 0].set(v)
    y_dn = _deltanet(pm3, ps, rows4, dn_conv, lane_vec(dn_a_log, DN_HEADS), lane_vec(dn_dt_bias, DN_HEADS),
                     row_vec(dn_a_log, DN_HEADS), row_vec(dn_dt_bias, DN_HEADS),
                     dn_norm.reshape(1, DN_HD), ltri, utri, shift)

    w2 = jnp.zeros((D_SMALL, GLA_KW), F32).at[n_small:n_small + GLA_RANK].set(gla_w2).astype(BF16)
    y_gla = _gla(pm3, ps, w2, gla_b.reshape(1, GLA_KW), gla_norm.reshape(1, GLA_VD), cmat)

    out = _out(x2, y_att.reshape(t, ATT_W), y_dn.reshape(t, DN_W), y_gla.reshape(t, GLA_W),
               p_l.reshape(t, D_PLE), w_out.astype(BF16), post_g.reshape(1, D_MODEL),
               ple_gate.astype(BF16), ple_proj.astype(BF16), ple_norm.reshape(1, D_MODEL),
               tm=_pick(t, (512, 256)))
    return out.reshape(b, s, D_MODEL)


def kernel(x, p, w_in, w_out, pre_gain, post_gain, att_lq1, att_lk1, att_lq2, att_lk2, att_subln, dn_conv, dn_a_log, dn_dt_bias, dn_norm, gla_w2, gla_b, gla_norm, ple_proj, ple_gate, ple_norm):
    consts = _chunk_constants()
    for i in range(p.shape[0]):
        x = _layer(x, p[i], w_in[i], w_out[i], pre_gain[i], post_gain[i], att_lq1[i], att_lk1[i],
                   att_lq2[i], att_lk2[i], att_subln[i], dn_conv[i], dn_a_log[i], dn_dt_bias[i],
                   dn_norm[i], gla_w2[i], gla_b[i], gla_norm[i], ple_proj[i], ple_gate[i],
                   ple_norm[i], i, consts)
    return x
```

```python
import functools
import math

import numpy as np
import jax
import jax.numpy as jnp
from jax import lax
from jax.experimental import pallas as pl
from jax.experimental.pallas import tpu as pltpu

F32 = jnp.float32
BF16 = jnp.bfloat16

D_MODEL = 1024
D_PLE = 256
EPS = 1e-6
ATT_HEADS = 8
ATT_HD = 64
ATT_VD = 128
ATT_W = 1024
DN_HEADS = 4
DN_HD = 128
DN_W = 512
DN_CONV = 4
GLA_HEADS = 4
GLA_KD = 64
GLA_VD = 128
GLA_KW = 256
GLA_W = 512
GLA_RANK = 16
GLA_TAU = 16.0
CHUNK = 64
D_MIX = 2048
D_MAIN = 6656
D_SMALL = 128
LANES = 128
VMEM_LIMIT = 48 * 1024 * 1024

NT = (((1,), (1,)), ((), ()))
TN = (((0,), (0,)), ((), ()))
GLA_LEVELS = (32, 16, 8, 4, 2, 1)
LOG2E = math.log2(math.e)
ATT_ONES = 16


def _sigmoid(x):
    return 1.0 / (1.0 + jnp.exp(-x))


def _softplus(x):
    return jnp.maximum(x, 0.0) + jnp.log(1.0 + jnp.exp(-jnp.abs(x)))


def _split_bf16(x):
    hi = x.astype(BF16)
    lo = (x - hi.astype(F32)).astype(BF16)
    return hi, lo


def _proj_body(x_ref, g_ref, wm_ref, ws_ref, wst_ref, wvt_ref, pm_ref, ps_ref, pst_ref, vt_ref, h_scr):
    @pl.when(pl.program_id(1) == 0)
    def _():
        x = x_ref[...]
        h = (x * lax.rsqrt(jnp.mean(x * x, axis=-1, keepdims=True) + EPS) * g_ref[...]).astype(BF16)
        h_scr[...] = h
        ps_ref[...] = jnp.dot(h, ws_ref[...], preferred_element_type=F32)
        pst_ref[...] = lax.dot_general(wst_ref[...], h, NT, preferred_element_type=F32)
        vt_ref[0] = lax.dot_general(wvt_ref[...], h, NT, preferred_element_type=F32).astype(BF16)

    pm_ref[...] = jnp.dot(h_scr[...], wm_ref[...], preferred_element_type=F32).astype(BF16)


def _proj(x2, g, wm, ws, wst, wvt, *, layer, b, s, tm, tn):
    t = x2.shape[0]
    spb = s // tm
    return pl.pallas_call(
        _proj_body,
        grid=(t // tm, D_MAIN // tn),
        in_specs=[
            pl.BlockSpec((tm, D_MODEL), lambda i, j: (i, 0)),
            pl.BlockSpec((1, D_MODEL), lambda i, j: (0, 0)),
            pl.BlockSpec((None, D_MODEL, tn), lambda i, j: (layer, 0, j)),
            pl.BlockSpec((None, D_MODEL, D_SMALL), lambda i, j: (layer, 0, 0)),
            pl.BlockSpec((None, 8, D_MODEL), lambda i, j: (layer, 0, 0)),
            pl.BlockSpec((None, ATT_W, D_MODEL), lambda i, j: (layer, 0, 0)),
        ],
        out_specs=[
            pl.BlockSpec((tm, tn), lambda i, j: (i, j)),
            pl.BlockSpec((tm, D_SMALL), lambda i, j: (i, 0)),
            pl.BlockSpec((8, tm), lambda i, j: (0, i)),
            pl.BlockSpec((1, ATT_W, tm), lambda i, j: (i // spb, 0, i % spb)),
        ],
        out_shape=[
            jax.ShapeDtypeStruct((t, D_MAIN), BF16),
            jax.ShapeDtypeStruct((t, D_SMALL), F32),
            jax.ShapeDtypeStruct((8, t), F32),
            jax.ShapeDtypeStruct((b, ATT_W, s), BF16),
        ],
        scratch_shapes=[pltpu.VMEM((tm, D_MODEL), BF16)],
        compiler_params=pltpu.CompilerParams(
            dimension_semantics=("parallel", "arbitrary"), vmem_limit_bytes=VMEM_LIMIT),
        name="proj",
    )(x2, g, wm, ws, wst, wvt)


def _attn_body(q_ref, k_ref, vt_ref, z_ref, lqk_ref, subln_ref, o_ref, acc_scr, *, tq, tk, n_q, lam_init):
    h = pl.program_id(1)
    sub = LANES
    nsub = tk // sub
    nch = tk // tq
    nr = 2 * tq
    lane = lax.broadcasted_iota(jnp.int32, (tq, LANES), 1)
    scale = ATT_HD ** -0.5 * LOG2E
    slope_row = LOG2E * jnp.exp2(jnp.full((1, nr), -8.0 / ATT_HEADS, F32) * (h + 1).astype(F32))
    krow = lax.broadcasted_iota(jnp.int32, (sub, nr), 0)
    qcol = lax.broadcasted_iota(jnp.int32, (sub, nr), 1)
    lag = krow - jnp.where(qcol >= tq, qcol - tq, qcol)
    bias = slope_row * krow.astype(F32)
    ones_rows = jnp.ones((ATT_ONES, tk), BF16)
    lqk = lqk_ref[...]
    lam = (jnp.exp(jnp.sum(lqk[0:1] * lqk[1:2], axis=-1, keepdims=True))
           - jnp.exp(jnp.sum(lqk[2:3] * lqk[3:4], axis=-1, keepdims=True)) + lam_init)

    chains = range(nch)

    def load_queries(qb):
        qs = []
        for c in chains:
            q = q_ref[0, qb * tk + c * tq:qb * tk + (c + 1) * tq, :].astype(F32) * scale
            qs.append(jnp.concatenate([jnp.where(lane < ATT_HD, q, 0.0),
                                       jnp.where(lane >= ATT_HD, q, 0.0)], axis=0).astype(BF16))
        return qs

    def key_count(diag):
        return [c + 1 if diag else nsub for c in chains]

    def scores(qs, j, diag):
        ns = key_count(diag)
        return [lax.dot_general(k_ref[0, j * tk:j * tk + ns[c] * sub, :], qs[c], NT,
                                preferred_element_type=F32) for c in chains]

    def update(qb, j, s_all, ms, diag):
        ns = key_count(diag)
        vt = jnp.concatenate([vt_ref[0, :, j * tk:(j + 1) * tk], ones_rows], axis=0)
        s, shift, ms_new = [], [], []
        for c in chains:
            sc, shc = [], []
            m_new = ms[c]
            for t in range(ns[c]):
                st = s_all[c][t * sub:(t + 1) * sub] + bias
                if diag and t == c:
                    st = jnp.where(lag <= 0, st, -1e30)
                sh = slope_row * float((j - qb) * tk + (t - c) * sub)
                m_new = jnp.maximum(m_new, jnp.max(st, axis=0, keepdims=True) + sh)
                sc.append(st)
                shc.append(sh)
            s.append(sc)
            shift.append(shc)
            ms_new.append(m_new)
        p = [jnp.concatenate([jnp.exp2((s[c][t] - (ms_new[c] - shift[c][t])).astype(BF16))
                              for t in range(ns[c])], axis=0) for c in chains]
        pv = [jnp.dot(vt[:, :ns[c] * sub], p[c], preferred_element_type=F32) for c in chains]
        for c in chains:
            if j == 0:
                acc_scr[c] = pv[c]
            else:
                acc_scr[c] = jnp.exp2(ms[c] - ms_new[c]) * acc_scr[c] + pv[c]
        return ms_new

    def finish(qb):
        for c in chains:
            rows = slice(qb * tk + c * tq, qb * tk + (c + 1) * tq)
            acc = acc_scr[c]
            ot = acc[:ATT_VD] * (1.0 / acc[ATT_VD:ATT_VD + 1])
            o = (ot[:, :tq] - lam * ot[:, tq:]).T
            o = o * lax.rsqrt(jnp.mean(o * o, axis=-1, keepdims=True) + EPS) * subln_ref[...] * (1.0 - lam_init)
            z = z_ref[0, rows, :].astype(F32)
            o_ref[0, rows, :] = (o * (z * _sigmoid(z))).astype(BF16)

    steps = [(qb, j) for qb in range(n_q // nch) for j in range(qb + 1)]
    qs = load_queries(0)
    s_next = scores(qs, 0, True)
    ms = None
    for n, (qb, j) in enumerate(steps):
        s_cur = s_next
        if n + 1 < len(steps):
            qb2, j2 = steps[n + 1]
            if qb2 != qb:
                qs = load_queries(qb2)
            s_next = scores(qs, j2, j2 == qb2)
        if j == 0:
            ms = [jnp.full((1, nr), -1e30, F32) for _ in chains]
        ms = update(qb, j, s_cur, ms, j == qb)
        if j == qb:
            finish(qb)


def _attn(pm3, vt, lqk, subln, *, tq, lam_init):
    b, s, _ = pm3.shape
    nq = ATT_W // LANES
    seq = lambda col: pl.BlockSpec((1, s, LANES), lambda bi, h: (bi, 0, col + h))
    tk = _pick(s, (512, 256, 128))
    return pl.pallas_call(
        functools.partial(_attn_body, tq=tq, tk=tk, n_q=s // tq, lam_init=lam_init),
        grid=(b, ATT_HEADS),
        in_specs=[
            seq(0), seq(nq),
            pl.BlockSpec((1, LANES, s), lambda bi, h: (bi, h, 0)),
            seq(2 * nq),
            pl.BlockSpec((4, ATT_HD), lambda bi, h: (0, 0)),
            pl.BlockSpec((1, ATT_VD), lambda bi, h: (0, 0)),
        ],
        out_specs=pl.BlockSpec((1, s, LANES), lambda bi, h: (bi, 0, h)),
        out_shape=jax.ShapeDtypeStruct((b, s, ATT_W), BF16),
        scratch_shapes=[pltpu.VMEM((tk // tq, ATT_VD + ATT_ONES, 2 * tq), F32)],
        compiler_params=pltpu.CompilerParams(
            dimension_semantics=("parallel", "parallel"), vmem_limit_bytes=VMEM_LIMIT),
        name="attn",
    )(pm3, pm3, vt, pm3, lqk, subln)


DN_GROUP = 2
DN_SUPER = 2


def _dn_body(dq_ref, dk_ref, dv_ref, dz_ref, ps_ref, rows_ref, cw_ref, acol_ref, dtcol_ref,
             arow_ref, dtrow_ref, norm_ref, ltri_ref, utri_ref, shift_ref, o_ref,
             st_scr, u_scr, wq_scr, kd_scr, a_scr, eg_scr, *, n_chunks):
    c_ = CHUNK
    gr = DN_GROUP * c_
    ii = lax.broadcasted_iota(jnp.int32, (c_, c_), 0)
    jj = lax.broadcasted_iota(jnp.int32, (c_, c_), 1)
    tri_incl = ii >= jj
    tri_strict = ii > jj
    eye = ii == jj
    cw = cw_ref[...]
    heads = range(DN_HEADS)
    lanes = [slice(h * DN_HD, (h + 1) * DN_HD) for h in heads]

    def conv_silu(win, w4):
        delayed = jnp.dot(shift_ref[...], win, preferred_element_type=F32)
        y = (delayed[0:gr] * w4[0:1] + delayed[gr:2 * gr] * w4[1:2] + delayed[2 * gr:3 * gr] * w4[2:3]
             + win[gr:2 * gr].astype(F32) * w4[3:4])
        return y * _sigmoid(y)

    def window(ref, r0, first):
        if first:
            return jnp.concatenate([jnp.zeros((gr, DN_W), BF16), ref[0, 0:gr, :]], axis=0)
        return ref[0, pl.ds(pl.multiple_of(r0 - gr, gr), 2 * gr), :]

    def mm(a, b):
        return jnp.dot(a.astype(BF16), b.astype(BF16), preferred_element_type=F32)

    def gather(gi, first):
        r0 = gi * gr if isinstance(gi, int) else pl.multiple_of(gi * gr, gr)
        cq = conv_silu(window(dq_ref, r0, first), cw[:, 0:DN_W])
        ck = conv_silu(window(dk_ref, r0, first), cw[:, DN_W:2 * DN_W])
        cv = conv_silu(window(dv_ref, r0, first), cw[:, 2 * DN_W:3 * DN_W])
        ps = ps_ref[pl.ds(r0, gr), :]
        beta_all = _sigmoid(ps)
        g_all = -jnp.exp(acol_ref[...]) * _softplus(ps + dtcol_ref[...])
        g_hi, g_lo = _split_bf16(g_all)
        chains = []
        for j in range(DN_GROUP):
            cidx = gi * DN_GROUP + j
            rs = slice(j * c_, (j + 1) * c_)
            gc_all = jnp.dot(ltri_ref[...], jnp.concatenate([g_hi[rs], g_lo[rs]], axis=0),
                             preferred_element_type=F32)
            g_row = -jnp.exp(arow_ref[...]) * _softplus(rows_ref[0, cidx] + dtrow_ref[...])
            r_hi, r_lo = _split_bf16(g_row)
            gc_rows = jnp.dot(jnp.concatenate([r_hi, r_lo], axis=1), utri_ref[...],
                              preferred_element_type=F32)
            for h in heads:
                qh, kh, vh = cq[rs, lanes[h]], ck[rs, lanes[h]], cv[rs, lanes[h]]
                qn = qh * (lax.rsqrt(jnp.sum(qh * qh, axis=-1, keepdims=True) + EPS) * DN_HD ** -0.5)
                kn = kh * lax.rsqrt(jnp.sum(kh * kh, axis=-1, keepdims=True) + EPS)
                beta = beta_all[rs, h:h + 1]
                gcc = gc_all[:, 4 + h:5 + h]
                gcr = gc_rows[4 + h:5 + h, :]
                gl = gcr[:, c_ - 1:c_]
                chains.append(dict(
                    cidx=cidx, h=h, rows=pl.ds(r0 + j * c_, c_), qn=qn, kn=kn, kb=kn * beta, vb=vh * beta,
                    gcc=gcc, gl=gl, decay=jnp.exp(jnp.where(tri_incl, gcc - gcr, -1e30))))
        return chains

    def prep(si, first):
        chains = []
        for sg in range(DN_SUPER):
            chains += gather(si * DN_SUPER + sg, first and sg == 0)
        for ch in chains:
            ch["gram"] = lax.dot_general(jnp.concatenate([ch["kb"], ch["qn"]], axis=0).astype(BF16),
                                         ch["kn"].astype(BF16), NT, preferred_element_type=F32)
        for ch in chains:
            x = jnp.where(tri_strict, -(ch["gram"][:c_] * ch["decay"]), 0.0)
            a_scr[ch["cidx"], ch["h"]] = (ch["gram"][c_:] * ch["decay"]).astype(BF16)
            ch["acc"] = jnp.where(eye, 1.0, x)
            ch["x"] = x
        for ch in chains:
            ch["xp"] = mm(ch["x"], ch["x"])
        for _ in range(4):
            for ch in chains:
                ch["both"] = mm(jnp.concatenate([ch["acc"], ch["xp"]], axis=0), ch["xp"])
            for ch in chains:
                ch["acc"] = ch["acc"] + ch["both"][:c_]
                ch["xp"] = ch["both"][c_:]
        for ch in chains:
            ch["last"] = mm(ch["acc"], ch["xp"])
        for ch in chains:
            t_inv = ch["acc"] + ch["last"]
            gexp = jnp.exp(ch["gcc"])
            ch["gexp"] = gexp
            ch["uw"] = mm(t_inv, jnp.concatenate([ch["vb"], ch["kb"] * gexp], axis=1))
        for ch in chains:
            cidx, h, rows = ch["cidx"], ch["h"], ch["rows"]
            u_scr[rows, lanes[h]] = ch["uw"][:, :DN_HD]
            wq_scr[cidx, 0:c_, lanes[h]] = ch["uw"][:, DN_HD:].astype(BF16)
            wq_scr[cidx, c_:2 * c_, lanes[h]] = (ch["qn"] * ch["gexp"]).astype(BF16)
            kd_scr[rows, lanes[h]] = (ch["kn"] * jnp.exp(ch["gl"] - ch["gcc"])).astype(BF16)
            eg_scr[cidx, h] = jnp.broadcast_to(jnp.exp(ch["gl"]), (8, DN_HD))

    prep(0, True)

    def prep_body(gi, carry):
        prep(gi, False)
        return carry

    lax.fori_loop(1, n_chunks // (DN_GROUP * DN_SUPER), prep_body, 0)

    st_scr[...] = jnp.zeros_like(st_scr)

    def scan(ci, carry):
        states = [st_scr[h] for h in heads]
        for u in range(DN_GROUP):
            c = ci * DN_GROUP + u
            rows = pl.ds(pl.multiple_of(c * c_, c_), c_)
            wq = [jnp.dot(wq_scr[c, :, lanes[h]], states[h].astype(BF16), preferred_element_type=F32)
                  for h in heads]
            v_new = [(u_scr[rows, lanes[h]] - wq[h][:c_]).astype(BF16) for h in heads]
            kv = [lax.dot_general(kd_scr[rows, lanes[h]], v_new[h], TN, preferred_element_type=F32)
                  for h in heads]
            o = [wq[h][c_:] + jnp.dot(a_scr[c, h], v_new[h], preferred_element_type=F32) for h in heads]
            states = [states[h] * eg_scr[c, h][0:1, :] + kv[h] for h in heads]
            for h in heads:
                on = o[h] * lax.rsqrt(jnp.mean(o[h] * o[h], axis=-1, keepdims=True) + EPS) * norm_ref[...]
                z = dz_ref[0, rows, lanes[h]].astype(F32)
                o_ref[0, rows, lanes[h]] = (on * (z * _sigmoid(z))).astype(BF16)
        for h in heads:
            st_scr[h] = states[h]
        return carry

    lax.fori_loop(0, n_chunks // DN_GROUP, scan, 0)


def _deltanet(pm3, ps, rows4, cw, acol, dtcol, arow, dtrow, norm, ltri, utri, shift):
    b, s, _ = pm3.shape
    n_chunks = s // CHUNK
    assert n_chunks % (DN_GROUP * DN_SUPER) == 0
    base = 3 * ATT_W // DN_W
    full = lambda shape: pl.BlockSpec(shape, lambda bi: (0,) * len(shape))
    return pl.pallas_call(
        functools.partial(_dn_body, n_chunks=n_chunks),
        grid=(b,),
        in_specs=[
            pl.BlockSpec((1, s, DN_W), lambda bi: (bi, 0, base)),
            pl.BlockSpec((1, s, DN_W), lambda bi: (bi, 0, base + 1)),
            pl.BlockSpec((1, s, DN_W), lambda bi: (bi, 0, base + 2)),
            pl.BlockSpec((1, s, DN_W), lambda bi: (bi, 0, base + 3)),
            pl.BlockSpec((s, D_SMALL), lambda bi: (bi, 0)),
            pl.BlockSpec((1, n_chunks, 8, CHUNK), lambda bi: (bi, 0, 0, 0)),
            full((DN_CONV, 3 * DN_W)),
            full((1, D_SMALL)), full((1, D_SMALL)), full((8, 1)), full((8, 1)),
            full((1, DN_HD)),
            full((CHUNK, 2 * CHUNK)), full((2 * CHUNK, CHUNK)), full(shift.shape),
        ],
        out_specs=pl.BlockSpec((1, s, DN_W), lambda bi: (bi, 0, 0)),
        out_shape=jax.ShapeDtypeStruct((b, s, DN_W), BF16),
        scratch_shapes=[
            pltpu.VMEM((DN_HEADS, DN_HD, DN_HD), F32),
            pltpu.VMEM((s, DN_W), F32),
            pltpu.VMEM((n_chunks, 2 * CHUNK, DN_W), BF16),
            pltpu.VMEM((s, DN_W), BF16),
            pltpu.VMEM((n_chunks, DN_HEADS, CHUNK, CHUNK), BF16),
            pltpu.VMEM((n_chunks, DN_HEADS, 8, DN_HD), F32),
        ],
        compiler_params=pltpu.CompilerParams(
            dimension_semantics=("parallel",), vmem_limit_bytes=VMEM_LIMIT),
        name="deltanet",
    )(pm3, pm3, pm3, pm3, ps, rows4, cw, acol, dtcol, arow, dtrow, norm, ltri, utri, shift)


GLA_GROUP = 2
GLA_PREP = 4


def _gla_body(gq_ref, gk_ref, gv_ref, gz_ref, ps_ref, w2_ref, b2_ref, norm_ref, cmat_ref,
              o_ref, st_scr, att_scr, qe_scr, ke_scr, dec_scr, *, n_chunks):
    c_ = CHUNK
    rowi = lax.broadcasted_iota(jnp.int32, (c_, 1), 0)
    upper = [(rowi % (2 * s)) >= s for s in GLA_LEVELS]
    ii = lax.broadcasted_iota(jnp.int32, (2 * c_, c_), 0) % c_
    jj = lax.broadcasted_iota(jnp.int32, (2 * c_, c_), 1)
    same = [(ii // (2 * s)) == (jj // (2 * s)) for s in GLA_LEVELS]
    eye = ii == jj
    lane = lax.broadcasted_iota(jnp.int32, (c_, LANES), 1)
    first_head = lane < GLA_KD
    lane_sq = lax.broadcasted_iota(jnp.int32, (GLA_VD, LANES), 1)
    head_lanes = [lane_sq < GLA_KD, lane_sq >= GLA_KD]

    def stack_heads(t):
        return jnp.concatenate([jnp.where(first_head, t, 0.0), jnp.where(first_head, 0.0, t)], axis=0)

    pairs = range(GLA_HEADS // 2)
    pair_lanes = [slice(p * LANES, (p + 1) * LANES) for p in pairs]
    gr = GLA_PREP * c_

    def prep(gi, carry):
        r0 = pl.multiple_of(gi * gr, gr)
        pre = (jnp.dot(ps_ref[pl.ds(r0, gr), :].astype(BF16), w2_ref[...], preferred_element_type=F32)
               + b2_ref[...])
        gk = (jnp.minimum(pre, 0.0) - jnp.log(1.0 + jnp.exp(-jnp.abs(pre)))) * (1.0 / GLA_TAU)
        g_hi, g_lo = _split_bf16(gk)
        chains = []
        for j in range(GLA_PREP):
            rs = slice(j * c_, (j + 1) * c_)
            rows = pl.ds(r0 + j * c_, c_)
            br = jnp.dot(cmat_ref[...], jnp.concatenate([g_hi[rs], g_lo[rs]], axis=0),
                         preferred_element_type=F32)
            for p in pairs:
                q = gq_ref[0, rows, pair_lanes[p]].astype(F32) * (GLA_KD ** -0.5)
                k = gk_ref[0, rows, pair_lanes[p]].astype(F32)
                chains.append(dict(cidx=gi * GLA_PREP + j, p=p, rows=rows, br=br, q=q, k=k,
                                   bcum=br[0:c_, pair_lanes[p]]))
        for ch in chains:
            ch["part"] = lax.dot_general(stack_heads(ch["q"]).astype(BF16), ch["k"].astype(BF16), NT,
                                         preferred_element_type=F32)
        for ch in chains:
            ch["att"] = jnp.where(eye, ch["part"], 0.0)
        for li in range(len(GLA_LEVELS)):
            up = upper[li]
            for ch in chains:
                ref = ch["br"][c_ * (li + 1):c_ * (li + 2), pair_lanes[ch["p"]]]
                e = jnp.exp(jnp.where(up, ch["bcum"] - ref, ref - ch["bcum"]))
                qt = jnp.where(up, ch["q"] * e, 0.0)
                kt = jnp.where(up, 0.0, ch["k"] * e)
                ch["part"] = lax.dot_general(stack_heads(qt).astype(BF16), kt.astype(BF16), NT,
                                             preferred_element_type=F32)
            for ch in chains:
                ch["att"] = ch["att"] + jnp.where(same[li], ch["part"], 0.0)
        for ch in chains:
            cidx, p, rows, bcum = ch["cidx"], ch["p"], ch["rows"], ch["bcum"]
            blast = bcum[c_ - 1:c_, :]
            att_scr[cidx, 2 * p] = ch["att"][:c_].astype(BF16)
            att_scr[cidx, 2 * p + 1] = ch["att"][c_:].astype(BF16)
            qe_scr[rows, pair_lanes[p]] = (ch["q"] * jnp.exp(bcum)).astype(BF16)
            ke_scr[rows, pair_lanes[p]] = (ch["k"] * jnp.exp(blast - bcum)).astype(BF16)
            dec_scr[cidx, p] = jnp.broadcast_to(jnp.exp(blast), (8, LANES))
        return carry

    lax.fori_loop(0, n_chunks // GLA_PREP, prep, 0)

    st_scr[...] = jnp.zeros_like(st_scr)
    heads = range(GLA_HEADS)
    head_cols = [slice(h * GLA_VD, (h + 1) * GLA_VD) for h in heads]

    def scan(ci, carry):
        states = [st_scr[h] for h in heads]
        for u in range(GLA_GROUP):
            c = ci * GLA_GROUP + u
            rows = pl.ds(pl.multiple_of(c * c_, c_), c_)
            qe = [qe_scr[rows, pair_lanes[p]] for p in pairs]
            ke = [ke_scr[rows, pair_lanes[p]] for p in pairs]
            vh = [gv_ref[0, rows, head_cols[h]] for h in heads]
            upd = [lax.dot_general(vh[h], ke[h // 2], TN, preferred_element_type=F32) for h in heads]
            intra = [jnp.dot(att_scr[c, h], vh[h], preferred_element_type=F32) for h in heads]
            inter = [lax.dot_general(qe[h // 2], states[h].astype(BF16), NT, preferred_element_type=F32)
                     for h in heads]
            states = [states[h] * dec_scr[c, h // 2][0:1, :] + jnp.where(head_lanes[h % 2], upd[h], 0.0)
                      for h in heads]
            for h in heads:
                o = inter[h] + intra[h]
                o = o * lax.rsqrt(jnp.mean(o * o, axis=-1, keepdims=True) + EPS) * norm_ref[...]
                z = gz_ref[0, rows, head_cols[h]].astype(F32)
                o_ref[0, rows, head_cols[h]] = (o * (z * _sigmoid(z))).astype(BF16)
        for h in heads:
            st_scr[h] = states[h]
        return carry

    lax.fori_loop(0, n_chunks // GLA_GROUP, scan, 0)


def _gla(pm3, ps, w2, b2, norm, cmat):
    b, s, _ = pm3.shape
    n_chunks = s // CHUNK
    off = 3 * ATT_W + 4 * DN_W
    full = lambda shape: pl.BlockSpec(shape, lambda bi: (0,) * len(shape))
    return pl.pallas_call(
        functools.partial(_gla_body, n_chunks=n_chunks),
        grid=(b,),
        in_specs=[
            pl.BlockSpec((1, s, GLA_KW), lambda bi: (bi, 0, off // GLA_KW)),
            pl.BlockSpec((1, s, GLA_KW), lambda bi: (bi, 0, off // GLA_KW + 1)),
            pl.BlockSpec((1, s, GLA_W), lambda bi: (bi, 0, (off + 2 * GLA_KW) // GLA_W)),
            pl.BlockSpec((1, s, GLA_W), lambda bi: (bi, 0, (off + 2 * GLA_KW) // GLA_W + 1)),
            pl.BlockSpec((s, D_SMALL), lambda bi: (bi, 0)),
            full((D_SMALL, GLA_KW)), full((1, GLA_KW)), full((1, GLA_VD)),
            full(((len(GLA_LEVELS) + 1) * CHUNK, 2 * CHUNK)),
        ],
        out_specs=pl.BlockSpec((1, s, GLA_W), lambda bi: (bi, 0, 0)),
        out_shape=jax.ShapeDtypeStruct((b, s, GLA_W), BF16),
        scratch_shapes=[
            pltpu.VMEM((GLA_HEADS, GLA_VD, LANES), F32),
            pltpu.VMEM((n_chunks, GLA_HEADS, CHUNK, CHUNK), BF16),
            pltpu.VMEM((s, GLA_KW), BF16),
            pltpu.VMEM((s, GLA_KW), BF16),
            pltpu.VMEM((n_chunks, GLA_HEADS // 2, 8, LANES), F32),
        ],
        compiler_params=pltpu.CompilerParams(
            dimension_semantics=("parallel",), vmem_limit_bytes=VMEM_LIMIT),
        name="gla",
    )(pm3, pm3, pm3, pm3, ps, w2, b2, norm, cmat)


def _out_body(x_ref, ya_ref, yd_ref, yg_ref, p_ref, wo_ref, postg_ref, pg_ref, pp_ref, pn_ref, o_ref):
    y = (jnp.dot(ya_ref[...], wo_ref[0:ATT_W, :], preferred_element_type=F32)
         + jnp.dot(yd_ref[...], wo_ref[ATT_W:ATT_W + DN_W, :], preferred_element_type=F32)
         + jnp.dot(yg_ref[...], wo_ref[ATT_W + DN_W:D_MIX, :], preferred_element_type=F32))
    x1 = x_ref[...] + y * lax.rsqrt(jnp.mean(y * y, axis=-1, keepdims=True) + EPS) * postg_ref[...]
    gate = _sigmoid(jnp.dot(x1.astype(BF16), pg_ref[...], preferred_element_type=F32))
    pe = jnp.dot(p_ref[...].astype(BF16), pp_ref[...], preferred_element_type=F32) * gate
    o_ref[...] = x1 + pe * lax.rsqrt(jnp.mean(pe * pe, axis=-1, keepdims=True) + EPS) * pn_ref[...]


def _out(x2, ya, yd, yg, p3, wo, postg, pg, pp, pn, *, layer, tm):
    t = x2.shape[0]
    tile = lambda w: pl.BlockSpec((tm, w), lambda i: (i, 0))
    full = lambda shape: pl.BlockSpec(shape, lambda i: (0, 0))
    stacked = lambda shape: pl.BlockSpec((None,) + shape, lambda i: (layer, 0, 0))
    return pl.pallas_call(
        _out_body,
        grid=(t // tm,),
        in_specs=[tile(D_MODEL), tile(ATT_W), tile(DN_W), tile(GLA_W),
                  pl.BlockSpec((None, tm, D_PLE), lambda i: (layer, i, 0)),
                  stacked((D_MIX, D_MODEL)), full((1, D_MODEL)), stacked((D_MODEL, D_MODEL)),
                  stacked((D_PLE, D_MODEL)), full((1, D_MODEL))],
        out_specs=tile(D_MODEL),
        out_shape=jax.ShapeDtypeStruct((t, D_MODEL), F32),
        compiler_params=pltpu.CompilerParams(
            dimension_semantics=("parallel",), vmem_limit_bytes=VMEM_LIMIT),
        name="out",
    )(x2, ya, yd, yg, p3, wo, postg, pg, pp, pn)


def _chunk_constants():
    c = CHUNK
    ltri = np.tril(np.ones((c, c), np.float32))
    blocks = [ltri]
    idx = np.arange(c)
    for s in GLA_LEVELS:
        bnd = (idx // (2 * s)) * (2 * s) + s - 1
        blocks.append(ltri[bnd])
    cmat = np.concatenate(blocks, axis=0)
    gr = DN_GROUP * c
    shift = np.zeros((DN_CONV - 1, gr, 2 * gr), np.float32)
    t = np.arange(gr)
    for i in range(DN_CONV - 1):
        shift[i, t, gr + t - (DN_CONV - 1) + i] = 1.0
    return (jnp.asarray(np.concatenate([ltri, ltri], axis=1), BF16),
            jnp.asarray(np.concatenate([ltri.T, ltri.T], axis=0), BF16),
            jnp.asarray(np.concatenate([cmat, cmat], axis=1), BF16),
            jnp.asarray(shift.reshape((DN_CONV - 1) * gr, 2 * gr), BF16))


def _pick(n, pref):
    for t in pref:
        if n % t == 0:
            return t
    return n


def _prepare_weights(w_in, w_out, ple_proj, ple_gate):
    n_big = 4 * ATT_W + 4 * DN_W
    n_small = 2 * DN_HEADS
    wm = jnp.concatenate([w_in[:, :, :2 * ATT_W], w_in[:, :, 3 * ATT_W:n_big],
                          w_in[:, :, n_big + n_small:n_big + n_small + 2 * GLA_KW + 2 * GLA_W]],
                         axis=2).astype(BF16)
    wvt = jnp.swapaxes(w_in[:, :, 2 * ATT_W:3 * ATT_W], 1, 2).astype(BF16)
    w_small = jnp.concatenate([w_in[:, :, n_big:n_big + n_small], w_in[:, :, -GLA_RANK:]], axis=2)
    ws = jnp.pad(w_small, ((0, 0), (0, 0), (0, D_SMALL - w_small.shape[2]))).astype(BF16)
    wst = jnp.swapaxes(w_small[:, :, :n_small], 1, 2).astype(BF16)
    return wm, ws, wst, wvt, w_out.astype(BF16), ple_gate.astype(BF16), ple_proj.astype(BF16)


def _layer(x, p3, weights, pre_g, post_g, lq1, lk1, lq2, lk2, att_subln, dn_conv, dn_a_log,
           dn_dt_bias, dn_norm, gla_w2, gla_b, gla_norm, ple_norm, layer_idx, consts):
    b, s, _ = x.shape
    t = b * s
    ltri, utri, cmat, shift = consts
    wm, ws, wst, wvt, wo, pg, pp = weights
    lam_init = 0.8 - 0.6 * math.exp(-0.3 * layer_idx)
    n_small = 2 * DN_HEADS

    x2 = x.reshape(t, D_MODEL)
    pm, ps, pst, vt = _proj(x2, pre_g.reshape(1, D_MODEL), wm, ws, wst, wvt, layer=layer_idx, b=b, s=s,
                            tm=_pick(s, (1024, 512, 256)), tn=_pick(D_MAIN, (1664, 512)))
    pm3 = pm.reshape(b, s, D_MAIN)

    lqk = jnp.stack([lq1, lk1, lq2, lk2], axis=0)
    y_att = _attn(pm3, vt, lqk, att_subln.reshape(1, ATT_VD), tq=LANES, lam_init=lam_init)

    rows4 = pst.reshape(8, b, s // CHUNK, CHUNK).transpose(1, 2, 0, 3)
    lane_vec = lambda v, off: jnp.zeros((1, D_SMALL), F32).at[0, off:off + v.shape[0]].set(v)
    row_vec = lambda v, off: jnp.zeros((8, 1), F32).at[off:off + v.shape[0], 0].set(v)
    y_dn = _deltanet(pm3, ps, rows4, dn_conv, lane_vec(dn_a_log, DN_HEADS), lane_vec(dn_dt_bias, DN_HEADS),
                     row_vec(dn_a_log, DN_HEADS), row_vec(dn_dt_bias, DN_HEADS),
                     dn_norm.reshape(1, DN_HD), ltri, utri, shift)

    w2 = jnp.zeros((D_SMALL, GLA_KW), F32).at[n_small:n_small + GLA_RANK].set(gla_w2).astype(BF16)
    y_gla = _gla(pm3, ps, w2, gla_b.reshape(1, GLA_KW), gla_norm.reshape(1, GLA_VD), cmat)

    out = _out(x2, y_att.reshape(t, ATT_W), y_dn.reshape(t, DN_W), y_gla.reshape(t, GLA_W), p3,
               wo, post_g.reshape(1, D_MODEL), pg, pp, ple_norm.reshape(1, D_MODEL),
               layer=layer_idx, tm=_pick(t, (512, 256)))
    return out.reshape(b, s, D_MODEL)


def kernel(x, p, w_in, w_out, pre_gain, post_gain, att_lq1, att_lk1, att_lq2, att_lk2, att_subln, dn_conv, dn_a_log, dn_dt_bias, dn_norm, gla_w2, gla_b, gla_norm, ple_proj, ple_gate, ple_norm):
    consts = _chunk_constants()
    weights = _prepare_weights(w_in, w_out, ple_proj, ple_gate)
    p3 = p.reshape(p.shape[0], -1, D_PLE)
    for i in range(p.shape[0]):
        x = _layer(x, p3, weights, pre_gain[i], post_gain[i], att_lq1[i], att_lk1[i],
                   att_lq2[i], att_lk2[i], att_subln[i], dn_conv[i], dn_a_log[i], dn_dt_bias[i],
                   dn_norm[i], gla_w2[i], gla_b[i], gla_norm[i], ple_norm[i], i, consts)
    return x
```

```python
import functools
import math

import numpy as np
import jax
import jax.numpy as jnp
from jax import lax
from jax.experimental import pallas as pl
from jax.experimental.pallas import tpu as pltpu

F32 = jnp.float32
BF16 = jnp.bfloat16

D_MODEL = 1024
D_PLE = 256
EPS = 1e-6
ATT_HEADS = 8
ATT_HD = 64
ATT_VD = 128
ATT_W = 1024
DN_HEADS = 4
DN_HD = 128
DN_W = 512
DN_CONV = 4
GLA_HEADS = 4
GLA_KD = 64
GLA_VD = 128
GLA_KW = 256
GLA_W = 512
GLA_RANK = 16
GLA_TAU = 16.0
CHUNK = 64
D_MIX = 2048
D_MAIN = 6656
D_SMALL = 128
LANES = 128
VMEM_LIMIT = 48 * 1024 * 1024

NT = (((1,), (1,)), ((), ()))
TN = (((0,), (0,)), ((), ()))
GLA_LEVELS = (32, 16, 8, 4, 2, 1)
LOG2E = math.log2(math.e)
ATT_ONES = 16


def _sigmoid(x):
    return 1.0 / (1.0 + jnp.exp(-x))


def _softplus(x):
    return jnp.maximum(x, 0.0) + jnp.log(1.0 + jnp.exp(-jnp.abs(x)))


def _split_bf16(x):
    hi = x.astype(BF16)
    lo = (x - hi.astype(F32)).astype(BF16)
    return hi, lo


def _proj_body(x_ref, g_ref, wm_ref, ws_ref, wst_ref, wvt_ref, pm_ref, ps_ref, pst_ref, vt_ref, h_scr):
    @pl.when(pl.program_id(1) == 0)
    def _():
        x = x_ref[...]
        h = (x * lax.rsqrt(jnp.mean(x * x, axis=-1, keepdims=True) + EPS) * g_ref[...]).astype(BF16)
        h_scr[...] = h
        ps_ref[...] = jnp.dot(h, ws_ref[...], preferred_element_type=F32)
        pst_ref[...] = lax.dot_general(wst_ref[...], h, NT, preferred_element_type=F32)
        vt_ref[0] = lax.dot_general(wvt_ref[...], h, NT, preferred_element_type=F32).astype(BF16)

    pm_ref[...] = jnp.dot(h_scr[...], wm_ref[...], preferred_element_type=F32).astype(BF16)


def _proj(x2, g, wm, ws, wst, wvt, *, layer, b, s, tm, tn):
    t = x2.shape[0]
    spb = s // tm
    return pl.pallas_call(
        _proj_body,
        grid=(t // tm, D_MAIN // tn),
        in_specs=[
            pl.BlockSpec((tm, D_MODEL), lambda i, j: (i, 0)),
            pl.BlockSpec((1, D_MODEL), lambda i, j: (0, 0)),
            pl.BlockSpec((None, D_MODEL, tn), lambda i, j: (layer, 0, j)),
            pl.BlockSpec((None, D_MODEL, D_SMALL), lambda i, j: (layer, 0, 0)),
            pl.BlockSpec((None, 8, D_MODEL), lambda i, j: (layer, 0, 0)),
            pl.BlockSpec((None, ATT_W, D_MODEL), lambda i, j: (layer, 0, 0)),
        ],
        out_specs=[
            pl.BlockSpec((tm, tn), lambda i, j: (i, j)),
            pl.BlockSpec((tm, D_SMALL), lambda i, j: (i, 0)),
            pl.BlockSpec((8, tm), lambda i, j: (0, i)),
            pl.BlockSpec((1, ATT_W, tm), lambda i, j: (i // spb, 0, i % spb)),
        ],
        out_shape=[
            jax.ShapeDtypeStruct((t, D_MAIN), BF16),
            jax.ShapeDtypeStruct((t, D_SMALL), F32),
            jax.ShapeDtypeStruct((8, t), F32),
            jax.ShapeDtypeStruct((b, ATT_W, s), BF16),
        ],
        scratch_shapes=[pltpu.VMEM((tm, D_MODEL), BF16)],
        compiler_params=pltpu.CompilerParams(
            dimension_semantics=("parallel", "arbitrary"), vmem_limit_bytes=VMEM_LIMIT),
        name="proj",
    )(x2, g, wm, ws, wst, wvt)


def _attn_body(q_ref, k_ref, vt_ref, z_ref, lqk_ref, subln_ref, o_ref, acc_scr, *, tq, tk, n_q, lam_init):
    h = pl.program_id(1)
    sub = LANES
    nsub = tk // sub
    nch = tk // tq
    nr = 2 * tq
    lane = lax.broadcasted_iota(jnp.int32, (tq, LANES), 1)
    scale = ATT_HD ** -0.5 * LOG2E
    slope_row = LOG2E * jnp.exp2(jnp.full((1, nr), -8.0 / ATT_HEADS, F32) * (h + 1).astype(F32))
    krow = lax.broadcasted_iota(jnp.int32, (sub, nr), 0)
    qcol = lax.broadcasted_iota(jnp.int32, (sub, nr), 1)
    lag = krow - jnp.where(qcol >= tq, qcol - tq, qcol)
    bias = slope_row * krow.astype(F32)
    ones_rows = jnp.ones((ATT_ONES, tk), BF16)
    lqk = lqk_ref[...]
    lam = (jnp.exp(jnp.sum(lqk[0:1] * lqk[1:2], axis=-1, keepdims=True))
           - jnp.exp(jnp.sum(lqk[2:3] * lqk[3:4], axis=-1, keepdims=True)) + lam_init)

    chains = range(nch)

    def load_queries(qb):
        qs = []
        for c in chains:
            q = q_ref[0, qb * tk + c * tq:qb * tk + (c + 1) * tq, :].astype(F32) * scale
            qs.append(jnp.concatenate([jnp.where(lane < ATT_HD, q, 0.0),
                                       jnp.where(lane >= ATT_HD, q, 0.0)], axis=0).astype(BF16))
        return qs

    def key_count(diag):
        return [c + 1 if diag else nsub for c in chains]

    def scores(qs, j, diag):
        ns = key_count(diag)
        return [lax.dot_general(k_ref[0, j * tk:j * tk + ns[c] * sub, :], qs[c], NT,
                                preferred_element_type=F32) for c in chains]

    def update(qb, j, s_all, ms, diag):
        ns = key_count(diag)
        vt = jnp.concatenate([vt_ref[0, :, j * tk:(j + 1) * tk], ones_rows], axis=0)
        s, shift, ms_new = [], [], []
        for c in chains:
            sc, shc = [], []
            m_new = ms[c]
            for t in range(ns[c]):
                st = s_all[c][t * sub:(t + 1) * sub] + bias
                if diag and t == c:
                    st = jnp.where(lag <= 0, st, -1e30)
                sh = slope_row * float((j - qb) * tk + (t - c) * sub)
                m_new = jnp.maximum(m_new, jnp.max(st, axis=0, keepdims=True) + sh)
                sc.append(st)
                shc.append(sh)
            s.append(sc)
            shift.append(shc)
            ms_new.append(m_new)
        p = [jnp.concatenate([jnp.exp2((s[c][t] - (ms_new[c] - shift[c][t])).astype(BF16))
                              for t in range(ns[c])], axis=0) for c in chains]
        pv = [jnp.dot(vt[:, :ns[c] * sub], p[c], preferred_element_type=F32) for c in chains]
        for c in chains:
            if j == 0:
                acc_scr[c] = pv[c]
            else:
                acc_scr[c] = jnp.exp2(ms[c] - ms_new[c]) * acc_scr[c] + pv[c]
        return ms_new

    def finish(qb):
        for c in chains:
            rows = slice(qb * tk + c * tq, qb * tk + (c + 1) * tq)
            acc = acc_scr[c]
            ot = acc[:ATT_VD] * (1.0 / acc[ATT_VD:ATT_VD + 1])
            o = (ot[:, :tq] - lam * ot[:, tq:]).T
            o = o * lax.rsqrt(jnp.mean(o * o, axis=-1, keepdims=True) + EPS) * subln_ref[...] * (1.0 - lam_init)
            z = z_ref[0, rows, :].astype(F32)
            o_ref[0, rows, :] = (o * (z * _sigmoid(z))).astype(BF16)

    steps = [(qb, j) for qb in range(n_q // nch) for j in range(qb + 1)]
    qs = load_queries(0)
    s_next = scores(qs, 0, True)
    ms = None
    for n, (qb, j) in enumerate(steps):
        s_cur = s_next
        if n + 1 < len(steps):
            qb2, j2 = steps[n + 1]
            if qb2 != qb:
                qs = load_queries(qb2)
            s_next = scores(qs, j2, j2 == qb2)
        if j == 0:
            ms = [jnp.full((1, nr), -1e30, F32) for _ in chains]
        ms = update(qb, j, s_cur, ms, j == qb)
        if j == qb:
            finish(qb)


def _attn(pm3, vt, lqk, subln, *, tq, lam_init):
    b, s, _ = pm3.shape
    nq = ATT_W // LANES
    seq = lambda col: pl.BlockSpec((1, s, LANES), lambda bi, h: (bi, 0, col + h))
    tk = _pick(s, (512, 256, 128))
    return pl.pallas_call(
        functools.partial(_attn_body, tq=tq, tk=tk, n_q=s // tq, lam_init=lam_init),
        grid=(b, ATT_HEADS),
        in_specs=[
            seq(0), seq(nq),
            pl.BlockSpec((1, LANES, s), lambda bi, h: (bi, h, 0)),
            seq(2 * nq),
            pl.BlockSpec((4, ATT_HD), lambda bi, h: (0, 0)),
            pl.BlockSpec((1, ATT_VD), lambda bi, h: (0, 0)),
        ],
        out_specs=pl.BlockSpec((1, s, LANES), lambda bi, h: (bi, 0, h)),
        out_shape=jax.ShapeDtypeStruct((b, s, ATT_W), BF16),
        scratch_shapes=[pltpu.VMEM((tk // tq, ATT_VD + ATT_ONES, 2 * tq), F32)],
        compiler_params=pltpu.CompilerParams(
            dimension_semantics=("parallel", "parallel"), vmem_limit_bytes=VMEM_LIMIT),
        name="attn",
    )(pm3, pm3, vt, pm3, lqk, subln)


DN_GROUP = 2
DN_SUPER = 2


def _dn_body(dq_ref, dk_ref, dv_ref, dz_ref, ps_ref, rows_ref, cw_ref, acol_ref, dtcol_ref,
             arow_ref, dtrow_ref, norm_ref, ltri_ref, utri_ref, shift_ref, o_ref,
             st_scr, u_scr, wq_scr, kd_scr, a_scr, eg_scr, *, n_chunks):
    c_ = CHUNK
    gr = DN_GROUP * c_
    ii = lax.broadcasted_iota(jnp.int32, (c_, c_), 0)
    jj = lax.broadcasted_iota(jnp.int32, (c_, c_), 1)
    tri_incl = ii >= jj
    tri_strict = ii > jj
    eye = ii == jj
    cw = cw_ref[...]
    heads = range(DN_HEADS)
    lanes = [slice(h * DN_HD, (h + 1) * DN_HD) for h in heads]

    def conv_silu(win, w4):
        delayed = jnp.dot(shift_ref[...], win, preferred_element_type=F32)
        y = (delayed[0:gr] * w4[0:1] + delayed[gr:2 * gr] * w4[1:2] + delayed[2 * gr:3 * gr] * w4[2:3]
             + win[gr:2 * gr].astype(F32) * w4[3:4])
        return y * _sigmoid(y)

    def window(ref, r0, first):
        if first:
            return jnp.concatenate([jnp.zeros((gr, DN_W), BF16), ref[0, 0:gr, :]], axis=0)
        return ref[0, pl.ds(pl.multiple_of(r0 - gr, gr), 2 * gr), :]

    def mm(a, b):
        return jnp.dot(a.astype(BF16), b.astype(BF16), preferred_element_type=F32)

    def gather(gi, first):
        r0 = gi * gr if isinstance(gi, int) else pl.multiple_of(gi * gr, gr)
        cq = conv_silu(window(dq_ref, r0, first), cw[:, 0:DN_W])
        ck = conv_silu(window(dk_ref, r0, first), cw[:, DN_W:2 * DN_W])
        cv = conv_silu(window(dv_ref, r0, first), cw[:, 2 * DN_W:3 * DN_W])
        ps = ps_ref[pl.ds(r0, gr), :]
        beta_all = _sigmoid(ps)
        g_all = -jnp.exp(acol_ref[...]) * _softplus(ps + dtcol_ref[...])
        g_hi, g_lo = _split_bf16(g_all)
        chains = []
        for j in range(DN_GROUP):
            cidx = gi * DN_GROUP + j
            rs = slice(j * c_, (j + 1) * c_)
            gc_all = jnp.dot(ltri_ref[...], jnp.concatenate([g_hi[rs], g_lo[rs]], axis=0),
                             preferred_element_type=F32)
            g_row = -jnp.exp(arow_ref[...]) * _softplus(rows_ref[0, cidx] + dtrow_ref[...])
            r_hi, r_lo = _split_bf16(g_row)
            gc_rows = jnp.dot(jnp.concatenate([r_hi, r_lo], axis=1), utri_ref[...],
                              preferred_element_type=F32)
            for h in heads:
                qh, kh, vh = cq[rs, lanes[h]], ck[rs, lanes[h]], cv[rs, lanes[h]]
                qn = qh * (lax.rsqrt(jnp.sum(qh * qh, axis=-1, keepdims=True) + EPS) * DN_HD ** -0.5)
                kn = kh * lax.rsqrt(jnp.sum(kh * kh, axis=-1, keepdims=True) + EPS)
                beta = beta_all[rs, h:h + 1]
                gcc = gc_all[:, 4 + h:5 + h]
                gcr = gc_rows[4 + h:5 + h, :]
                gl = gcr[:, c_ - 1:c_]
                chains.append(dict(
                    cidx=cidx, h=h, rows=pl.ds(r0 + j * c_, c_), qn=qn, kn=kn, kb=kn * beta, vb=vh * beta,
                    gcc=gcc, gl=gl, decay=jnp.exp(jnp.where(tri_incl, gcc - gcr, -1e30))))
        return chains

    def prep(si, first):
        chains = []
        for sg in range(DN_SUPER):
            chains += gather(si * DN_SUPER + sg, first and sg == 0)
        for ch in chains:
            ch["gram"] = lax.dot_general(jnp.concatenate([ch["kb"], ch["qn"]], axis=0).astype(BF16),
                                         ch["kn"].astype(BF16), NT, preferred_element_type=F32)
        for ch in chains:
            x = jnp.where(tri_strict, -(ch["gram"][:c_] * ch["decay"]), 0.0)
            a_scr[ch["cidx"], ch["h"]] = (ch["gram"][c_:] * ch["decay"]).astype(BF16)
            ch["acc"] = jnp.where(eye, 1.0, x)
            ch["x"] = x
        for ch in chains:
            ch["xp"] = mm(ch["x"], ch["x"])
        for _ in range(4):
            for ch in chains:
                ch["both"] = mm(jnp.concatenate([ch["acc"], ch["xp"]], axis=0), ch["xp"])
            for ch in chains:
                ch["acc"] = ch["acc"] + ch["both"][:c_]
                ch["xp"] = ch["both"][c_:]
        for ch in chains:
            ch["last"] = mm(ch["acc"], ch["xp"])
        for ch in chains:
            t_inv = ch["acc"] + ch["last"]
            gexp = jnp.exp(ch["gcc"])
            ch["gexp"] = gexp
            ch["uw"] = mm(t_inv, jnp.concatenate([ch["vb"], ch["kb"] * gexp], axis=1))
        for ch in chains:
            cidx, h, rows = ch["cidx"], ch["h"], ch["rows"]
            u_scr[rows, lanes[h]] = ch["uw"][:, :DN_HD]
            wq_scr[cidx, 0:c_, lanes[h]] = ch["uw"][:, DN_HD:].astype(BF16)
            wq_scr[cidx, c_:2 * c_, lanes[h]] = (ch["qn"] * ch["gexp"]).astype(BF16)
            kd_scr[rows, lanes[h]] = (ch["kn"] * jnp.exp(ch["gl"] - ch["gcc"])).astype(BF16)
            eg_scr[cidx, h] = jnp.broadcast_to(jnp.exp(ch["gl"]), (8, DN_HD))

    prep(0, True)

    def prep_body(gi, carry):
        prep(gi, False)
        return carry

    lax.fori_loop(1, n_chunks // (DN_GROUP * DN_SUPER), prep_body, 0)

    st_scr[...] = jnp.zeros_like(st_scr)

    def scan(ci, carry):
        states = [st_scr[h] for h in heads]
        for u in range(DN_GROUP):
            c = ci * DN_GROUP + u
            rows = pl.ds(pl.multiple_of(c * c_, c_), c_)
            wq = [jnp.dot(wq_scr[c, :, lanes[h]], states[h].astype(BF16), preferred_element_type=F32)
                  for h in heads]
            v_new = [(u_scr[rows, lanes[h]] - wq[h][:c_]).astype(BF16) for h in heads]
            kv = [lax.dot_general(kd_scr[rows, lanes[h]], v_new[h], TN, preferred_element_type=F32)
                  for h in heads]
            o = [wq[h][c_:] + jnp.dot(a_scr[c, h], v_new[h], preferred_element_type=F32) for h in heads]
            states = [states[h] * eg_scr[c, h][0:1, :] + kv[h] for h in heads]
            for h in heads:
                on = o[h] * lax.rsqrt(jnp.mean(o[h] * o[h], axis=-1, keepdims=True) + EPS) * norm_ref[...]
                z = dz_ref[0, rows, lanes[h]].astype(F32)
                o_ref[0, rows, lanes[h]] = (on * (z * _sigmoid(z))).astype(BF16)
        for h in heads:
            st_scr[h] = states[h]
        return carry

    lax.fori_loop(0, n_chunks // DN_GROUP, scan, 0)


def _deltanet(pm3, ps, rows4, cw, acol, dtcol, arow, dtrow, norm, ltri, utri, shift):
    b, s, _ = pm3.shape
    n_chunks = s // CHUNK
    assert n_chunks % (DN_GROUP * DN_SUPER) == 0
    base = 3 * ATT_W // DN_W
    full = lambda shape: pl.BlockSpec(shape, lambda bi: (0,) * len(shape))
    return pl.pallas_call(
        functools.partial(_dn_body, n_chunks=n_chunks),
        grid=(b,),
        in_specs=[
            pl.BlockSpec((1, s, DN_W), lambda bi: (bi, 0, base)),
            pl.BlockSpec((1, s, DN_W), lambda bi: (bi, 0, base + 1)),
            pl.BlockSpec((1, s, DN_W), lambda bi: (bi, 0, base + 2)),
            pl.BlockSpec((1, s, DN_W), lambda bi: (bi, 0, base + 3)),
            pl.BlockSpec((s, D_SMALL), lambda bi: (bi, 0)),
            pl.BlockSpec((1, n_chunks, 8, CHUNK), lambda bi: (bi, 0, 0, 0)),
            full((DN_CONV, 3 * DN_W)),
            full((1, D_SMALL)), full((1, D_SMALL)), full((8, 1)), full((8, 1)),
            full((1, DN_HD)),
            full((CHUNK, 2 * CHUNK)), full((2 * CHUNK, CHUNK)), full(shift.shape),
        ],
        out_specs=pl.BlockSpec((1, s, DN_W), lambda bi: (bi, 0, 0)),
        out_shape=jax.ShapeDtypeStruct((b, s, DN_W), BF16),
        scratch_shapes=[
            pltpu.VMEM((DN_HEADS, DN_HD, DN_HD), F32),
            pltpu.VMEM((s, DN_W), F32),
            pltpu.VMEM((n_chunks, 2 * CHUNK, DN_W), BF16),
            pltpu.VMEM((s, DN_W), BF16),
            pltpu.VMEM((n_chunks, DN_HEADS, CHUNK, CHUNK), BF16),
            pltpu.VMEM((n_chunks, DN_HEADS, 8, DN_HD), F32),
        ],
        compiler_params=pltpu.CompilerParams(
            dimension_semantics=("parallel",), vmem_limit_bytes=VMEM_LIMIT),
        name="deltanet",
    )(pm3, pm3, pm3, pm3, ps, rows4, cw, acol, dtcol, arow, dtrow, norm, ltri, utri, shift)


GLA_GROUP = 2
GLA_PREP = 4


def _gla_body(gq_ref, gk_ref, gv_ref, gz_ref, ps_ref, w2_ref, b2_ref, norm_ref, cmat_ref,
              o_ref, st_scr, att_scr, qe_scr, ke_scr, dec_scr, *, n_chunks):
    c_ = CHUNK
    rowi = lax.broadcasted_iota(jnp.int32, (c_, 1), 0)
    upper = [(rowi % (2 * s)) >= s for s in GLA_LEVELS]
    ii = lax.broadcasted_iota(jnp.int32, (2 * c_, c_), 0) % c_
    jj = lax.broadcasted_iota(jnp.int32, (2 * c_, c_), 1)
    same = [(ii // (2 * s)) == (jj // (2 * s)) for s in GLA_LEVELS]
    eye = ii == jj
    lane = lax.broadcasted_iota(jnp.int32, (c_, LANES), 1)
    first_head = lane < GLA_KD
    lane_sq = lax.broadcasted_iota(jnp.int32, (GLA_VD, LANES), 1)
    head_lanes = [lane_sq < GLA_KD, lane_sq >= GLA_KD]

    def stack_heads(t):
        return jnp.concatenate([jnp.where(first_head, t, 0.0), jnp.where(first_head, 0.0, t)], axis=0)

    pairs = range(GLA_HEADS // 2)
    pair_lanes = [slice(p * LANES, (p + 1) * LANES) for p in pairs]
    gr = GLA_PREP * c_

    def prep(gi, carry):
        r0 = pl.multiple_of(gi * gr, gr)
        pre = (jnp.dot(ps_ref[pl.ds(r0, gr), :].astype(BF16), w2_ref[...], preferred_element_type=F32)
               + b2_ref[...])
        gk = (jnp.minimum(pre, 0.0) - jnp.log(1.0 + jnp.exp(-jnp.abs(pre)))) * (1.0 / GLA_TAU)
        g_hi, g_lo = _split_bf16(gk)
        chains = []
        for j in range(GLA_PREP):
            rs = slice(j * c_, (j + 1) * c_)
            rows = pl.ds(r0 + j * c_, c_)
            br = jnp.dot(cmat_ref[...], jnp.concatenate([g_hi[rs], g_lo[rs]], axis=0),
                         preferred_element_type=F32)
            for p in pairs:
                q = gq_ref[0, rows, pair_lanes[p]].astype(F32) * (GLA_KD ** -0.5)
                k = gk_ref[0, rows, pair_lanes[p]].astype(F32)
                chains.append(dict(cidx=gi * GLA_PREP + j, p=p, rows=rows, br=br, q=q, k=k,
                                   bcum=br[0:c_, pair_lanes[p]]))
        for ch in chains:
            ch["part"] = lax.dot_general(stack_heads(ch["q"]).astype(BF16), ch["k"].astype(BF16), NT,
                                         preferred_element_type=F32)
        for ch in chains:
            ch["att"] = jnp.where(eye, ch["part"], 0.0)
        for li in range(len(GLA_LEVELS)):
            up = upper[li]
            for ch in chains:
                ref = ch["br"][c_ * (li + 1):c_ * (li + 2), pair_lanes[ch["p"]]]
                e = jnp.exp(jnp.where(up, ch["bcum"] - ref, ref - ch["bcum"]))
                qt = jnp.where(up, ch["q"] * e, 0.0)
                kt = jnp.where(up, 0.0, ch["k"] * e)
                ch["part"] = lax.dot_general(stack_heads(qt).astype(BF16), kt.astype(BF16), NT,
                                             preferred_element_type=F32)
            for ch in chains:
                ch["att"] = ch["att"] + jnp.where(same[li], ch["part"], 0.0)
        for ch in chains:
            cidx, p, rows, bcum = ch["cidx"], ch["p"], ch["rows"], ch["bcum"]
            blast = bcum[c_ - 1:c_, :]
            att_scr[cidx, 2 * p] = ch["att"][:c_].astype(BF16)
            att_scr[cidx, 2 * p + 1] = ch["att"][c_:].astype(BF16)
            qe_scr[rows, pair_lanes[p]] = (ch["q"] * jnp.exp(bcum)).astype(BF16)
            ke_scr[rows, pair_lanes[p]] = (ch["k"] * jnp.exp(blast - bcum)).astype(BF16)
            dec_scr[cidx, p] = jnp.broadcast_to(jnp.exp(blast), (8, LANES))
        return carry

    lax.fori_loop(0, n_chunks // GLA_PREP, prep, 0)

    st_scr[...] = jnp.zeros_like(st_scr)
    heads = range(GLA_HEADS)
    head_cols = [slice(h * GLA_VD, (h + 1) * GLA_VD) for h in heads]

    def scan(ci, carry):
        states = [st_scr[h] for h in heads]
        for u in range(GLA_GROUP):
            c = ci * GLA_GROUP + u
            rows = pl.ds(pl.multiple_of(c * c_, c_), c_)
            qe = [qe_scr[rows, pair_lanes[p]] for p in pairs]
            ke = [ke_scr[rows, pair_lanes[p]] for p in pairs]
            vh = [gv_ref[0, rows, head_cols[h]] for h in heads]
            upd = [lax.dot_general(vh[h], ke[h // 2], TN, preferred_element_type=F32) for h in heads]
            intra = [jnp.dot(att_scr[c, h], vh[h], preferred_element_type=F32) for h in heads]
            inter = [lax.dot_general(qe[h // 2], states[h].astype(BF16), NT, preferred_element_type=F32)
                     for h in heads]
            states = [states[h] * dec_scr[c, h // 2][0:1, :] + jnp.where(head_lanes[h % 2], upd[h], 0.0)
                      for h in heads]
            for h in heads:
                o = inter[h] + intra[h]
                o = o * lax.rsqrt(jnp.mean(o * o, axis=-1, keepdims=True) + EPS) * norm_ref[...]
                z = gz_ref[0, rows, head_cols[h]].astype(F32)
                o_ref[0, rows, head_cols[h]] = (o * (z * _sigmoid(z))).astype(BF16)
        for h in heads:
            st_scr[h] = states[h]
        return carry

    lax.fori_loop(0, n_chunks // GLA_GROUP, scan, 0)


def _gla(pm3, ps, w2, b2, norm, cmat):
    b, s, _ = pm3.shape
    n_chunks = s // CHUNK
    off = 3 * ATT_W + 4 * DN_W
    full = lambda shape: pl.BlockSpec(shape, lambda bi: (0,) * len(shape))
    return pl.pallas_call(
        functools.partial(_gla_body, n_chunks=n_chunks),
        grid=(b,),
        in_specs=[
            pl.BlockSpec((1, s, GLA_KW), lambda bi: (bi, 0, off // GLA_KW)),
            pl.BlockSpec((1, s, GLA_KW), lambda bi: (bi, 0, off // GLA_KW + 1)),
            pl.BlockSpec((1, s, GLA_W), lambda bi: (bi, 0, (off + 2 * GLA_KW) // GLA_W)),
            pl.BlockSpec((1, s, GLA_W), lambda bi: (bi, 0, (off + 2 * GLA_KW) // GLA_W + 1)),
            pl.BlockSpec((s, D_SMALL), lambda bi: (bi, 0)),
            full((D_SMALL, GLA_KW)), full((1, GLA_KW)), full((1, GLA_VD)),
            full(((len(GLA_LEVELS) + 1) * CHUNK, 2 * CHUNK)),
        ],
        out_specs=pl.BlockSpec((1, s, GLA_W), lambda bi: (bi, 0, 0)),
        out_shape=jax.ShapeDtypeStruct((b, s, GLA_W), BF16),
        scratch_shapes=[
            pltpu.VMEM((GLA_HEADS, GLA_VD, LANES), F32),
            pltpu.VMEM((n_chunks, GLA_HEADS, CHUNK, CHUNK), BF16),
            pltpu.VMEM((s, GLA_KW), BF16),
            pltpu.VMEM((s, GLA_KW), BF16),
            pltpu.VMEM((n_chunks, GLA_HEADS // 2, 8, LANES), F32),
        ],
        compiler_params=pltpu.CompilerParams(
            dimension_semantics=("parallel",), vmem_limit_bytes=VMEM_LIMIT),
        name="gla",
    )(pm3, pm3, pm3, pm3, ps, w2, b2, norm, cmat)


def _out_body(x_ref, ya_ref, yd_ref, yg_ref, p_ref, wo_ref, postg_ref, pg_ref, pp_ref, pn_ref, o_ref):
    y = (jnp.dot(ya_ref[...], wo_ref[0:ATT_W, :], preferred_element_type=F32)
         + jnp.dot(yd_ref[...], wo_ref[ATT_W:ATT_W + DN_W, :], preferred_element_type=F32)
         + jnp.dot(yg_ref[...], wo_ref[ATT_W + DN_W:D_MIX, :], preferred_element_type=F32))
    x1 = x_ref[...] + y * lax.rsqrt(jnp.mean(y * y, axis=-1, keepdims=True) + EPS) * postg_ref[...]
    gate = _sigmoid(jnp.dot(x1.astype(BF16), pg_ref[...], preferred_element_type=F32))
    pe = jnp.dot(p_ref[...].astype(BF16), pp_ref[...], preferred_element_type=F32) * gate
    o_ref[...] = x1 + pe * lax.rsqrt(jnp.mean(pe * pe, axis=-1, keepdims=True) + EPS) * pn_ref[...]


def _out(x2, ya, yd, yg, p3, wo, postg, pg, pp, pn, *, layer, tm):
    t = x2.shape[0]
    tile = lambda w: pl.BlockSpec((tm, w), lambda i: (i, 0))
    full = lambda shape: pl.BlockSpec(shape, lambda i: (0, 0))
    stacked = lambda shape: pl.BlockSpec((None,) + shape, lambda i: (layer, 0, 0))
    return pl.pallas_call(
        _out_body,
        grid=(t // tm,),
        in_specs=[tile(D_MODEL), tile(ATT_W), tile(DN_W), tile(GLA_W),
                  pl.BlockSpec((None, tm, D_PLE), lambda i: (layer, i, 0)),
                  stacked((D_MIX, D_MODEL)), full((1, D_MODEL)), stacked((D_MODEL, D_MODEL)),
                  stacked((D_PLE, D_MODEL)), full((1, D_MODEL))],
        out_specs=tile(D_MODEL),
        out_shape=jax.ShapeDtypeStruct((t, D_MODEL), F32),
        compiler_params=pltpu.CompilerParams(
            dimension_semantics=("parallel",), vmem_limit_bytes=VMEM_LIMIT),
        name="out",
    )(x2, ya, yd, yg, p3, wo, postg, pg, pp, pn)


def _chunk_constants():
    c = CHUNK
    ltri = np.tril(np.ones((c, c), np.float32))
    blocks = [ltri]
    idx = np.arange(c)
    for s in GLA_LEVELS:
        bnd = (idx // (2 * s)) * (2 * s) + s - 1
        blocks.append(ltri[bnd])
    cmat = np.concatenate(blocks, axis=0)
    gr = DN_GROUP * c
    shift = np.zeros((DN_CONV - 1, gr, 2 * gr), np.float32)
    t = np.arange(gr)
    for i in range(DN_CONV - 1):
        shift[i, t, gr + t - (DN_CONV - 1) + i] = 1.0
    return (jnp.asarray(np.concatenate([ltri, ltri], axis=1), BF16),
            jnp.asarray(np.concatenate([ltri.T, ltri.T], axis=0), BF16),
            jnp.asarray(np.concatenate([cmat, cmat], axis=1), BF16),
            jnp.asarray(shift.reshape((DN_CONV - 1) * gr, 2 * gr), BF16))


def _pick(n, pref):
    for t in pref:
        if n % t == 0:
            return t
    return n


def _prepare_weights(w_in, w_out, ple_proj, ple_gate):
    n_big = 4 * ATT_W + 4 * DN_W
    n_small = 2 * DN_HEADS
    wm = jnp.concatenate([w_in[:, :, :2 * ATT_W], w_in[:, :, 3 * ATT_W:n_big],
                          w_in[:, :, n_big + n_small:n_big + n_small + 2 * GLA_KW + 2 * GLA_W]],
                         axis=2).astype(BF16)
    wvt = jnp.swapaxes(w_in[:, :, 2 * ATT_W:3 * ATT_W], 1, 2).astype(BF16)
    w_small = jnp.concatenate([w_in[:, :, n_big:n_big + n_small], w_in[:, :, -GLA_RANK:]], axis=2)
    ws = jnp.pad(w_small, ((0, 0), (0, 0), (0, D_SMALL - w_small.shape[2]))).astype(BF16)
    wst = jnp.swapaxes(w_small[:, :, :n_small], 1, 2).astype(BF16)
    return wm, ws, wst, wvt, w_out.astype(BF16), ple_gate.astype(BF16), ple_proj.astype(BF16)


def _layer(x, p3, weights, pre_g, post_g, lq1, lk1, lq2, lk2, att_subln, dn_conv, dn_a_log,
           dn_dt_bias, dn_norm, gla_w2, gla_b, gla_norm, ple_norm, layer_idx, consts):
    b, s, _ = x.shape
    t = b * s
    ltri, utri, cmat, shift = consts
    wm, ws, wst, wvt, wo, pg, pp = weights
    lam_init = 0.8 - 0.6 * math.exp(-0.3 * layer_idx)
    n_small = 2 * DN_HEADS

    x2 = x.reshape(t, D_MODEL)
    pm, ps, pst, vt = _proj(x2, pre_g.reshape(1, D_MODEL), wm, ws, wst, wvt, layer=layer_idx, b=b, s=s,
                            tm=_pick(s, (1024, 512, 256)), tn=_pick(D_MAIN, (3328, 512)))
    pm3 = pm.reshape(b, s, D_MAIN)

    lqk = jnp.stack([lq1, lk1, lq2, lk2], axis=0)
    y_att = _attn(pm3, vt, lqk, att_subln.reshape(1, ATT_VD), tq=LANES, lam_init=lam_init)

    rows4 = pst.reshape(8, b, s // CHUNK, CHUNK).transpose(1, 2, 0, 3)
    lane_vec = lambda v, off: jnp.zeros((1, D_SMALL), F32).at[0, off:off + v.shape[0]].set(v)
    row_vec = lambda v, off: jnp.zeros((8, 1), F32).at[off:off + v.shape[0], 0].set(v)
    y_dn = _deltanet(pm3, ps, rows4, dn_conv, lane_vec(dn_a_log, DN_HEADS), lane_vec(dn_dt_bias, DN_HEADS),
                     row_vec(dn_a_log, DN_HEADS), row_vec(dn_dt_bias, DN_HEADS),
                     dn_norm.reshape(1, DN_HD), ltri, utri, shift)

    w2 = jnp.zeros((D_SMALL, GLA_KW), F32).at[n_small:n_small + GLA_RANK].set(gla_w2).astype(BF16)
    y_gla = _gla(pm3, ps, w2, gla_b.reshape(1, GLA_KW), gla_norm.reshape(1, GLA_VD), cmat)

    out = _out(x2, y_att.reshape(t, ATT_W), y_dn.reshape(t, DN_W), y_gla.reshape(t, GLA_W), p3,
               wo, post_g.reshape(1, D_MODEL), pg, pp, ple_norm.reshape(1, D_MODEL),
               layer=layer_idx, tm=_pick(t, (512, 256)))
    return out.reshape(b, s, D_MODEL)


def kernel(x, p, w_in, w_out, pre_gain, post_gain, att_lq1, att_lk1, att_lq2, att_lk2, att_subln, dn_conv, dn_a_log, dn_dt_bias, dn_norm, gla_w2, gla_b, gla_norm, ple_proj, ple_gate, ple_norm):
    consts = _chunk_constants()
    weights = _prepare_weights(w_in, w_out, ple_proj, ple_gate)
    p3 = p.reshape(p.shape[0], -1, D_PLE)
    for i in range(p.shape[0]):
        x = _layer(x, p3, weights, pre_gain[i], post_gain[i], att_lq1[i], att_lk1[i],
                   att_lq2[i], att_lk2[i], att_subln[i], dn_conv[i], dn_a_log[i], dn_dt_bias[i],
                   dn_norm[i], gla_w2[i], gla_b[i], gla_norm[i], ple_norm[i], i, consts)
    return x
```

```python
import functools
import math

import numpy as np
import jax
import jax.numpy as jnp
from jax import lax
from jax.experimental import pallas as pl
from jax.experimental.pallas import tpu as pltpu

F32 = jnp.float32
BF16 = jnp.bfloat16

D_MODEL = 1024
D_PLE = 256
EPS = 1e-6
ATT_HEADS = 8
ATT_HD = 64
ATT_VD = 128
ATT_W = 1024
DN_HEADS = 4
DN_HD = 128
DN_W = 512
DN_CONV = 4
GLA_HEADS = 4
GLA_KD = 64
GLA_VD = 128
GLA_KW = 256
GLA_W = 512
GLA_RANK = 16
GLA_TAU = 16.0
CHUNK = 64
D_MIX = 2048
D_MAIN = 6656
D_SMALL = 128
LANES = 128
VMEM_LIMIT = 48 * 1024 * 1024

NT = (((1,), (1,)), ((), ()))
TN = (((0,), (0,)), ((), ()))
GLA_LEVELS = (32, 16, 8, 4, 2, 1)
LOG2E = math.log2(math.e)
ATT_ONES = 16


def _sigmoid(x):
    return 1.0 / (1.0 + jnp.exp(-x))


def _softplus(x):
    return jnp.maximum(x, 0.0) + jnp.log(1.0 + jnp.exp(-jnp.abs(x)))


def _split_bf16(x):
    hi = x.astype(BF16)
    lo = (x - hi.astype(F32)).astype(BF16)
    return hi, lo


def _proj_body(x_ref, g_ref, wm_ref, ws_ref, wst_ref, wvt_ref, pm_ref, ps_ref, pst_ref, vt_ref, h_scr):
    @pl.when(pl.program_id(1) == 0)
    def _():
        x = x_ref[...]
        h = (x * lax.rsqrt(jnp.mean(x * x, axis=-1, keepdims=True) + EPS) * g_ref[...]).astype(BF16)
        h_scr[...] = h
        ps_ref[...] = jnp.dot(h, ws_ref[...], preferred_element_type=F32)
        pst_ref[...] = lax.dot_general(wst_ref[...], h, NT, preferred_element_type=F32)
        vt_ref[0] = lax.dot_general(wvt_ref[...], h, NT, preferred_element_type=F32).astype(BF16)

    pm_ref[...] = lax.dot_general(h_scr[...], wm_ref[...], NT, preferred_element_type=F32).astype(BF16)


def _proj(x2, g, wm, ws, wst, wvt, *, layer, b, s, tm, tn):
    t = x2.shape[0]
    spb = s // tm
    return pl.pallas_call(
        _proj_body,
        grid=(t // tm, D_MAIN // tn),
        in_specs=[
            pl.BlockSpec((tm, D_MODEL), lambda i, j: (i, 0)),
            pl.BlockSpec((1, D_MODEL), lambda i, j: (0, 0)),
            pl.BlockSpec((None, tn, D_MODEL), lambda i, j: (layer, j, 0)),
            pl.BlockSpec((None, D_MODEL, D_SMALL), lambda i, j: (layer, 0, 0)),
            pl.BlockSpec((None, 8, D_MODEL), lambda i, j: (layer, 0, 0)),
            pl.BlockSpec((None, ATT_W, D_MODEL), lambda i, j: (layer, 0, 0)),
        ],
        out_specs=[
            pl.BlockSpec((tm, tn), lambda i, j: (i, j)),
            pl.BlockSpec((tm, D_SMALL), lambda i, j: (i, 0)),
            pl.BlockSpec((8, tm), lambda i, j: (0, i)),
            pl.BlockSpec((1, ATT_W, tm), lambda i, j: (i // spb, 0, i % spb)),
        ],
        out_shape=[
            jax.ShapeDtypeStruct((t, D_MAIN), BF16),
            jax.ShapeDtypeStruct((t, D_SMALL), F32),
            jax.ShapeDtypeStruct((8, t), F32),
            jax.ShapeDtypeStruct((b, ATT_W, s), BF16),
        ],
        scratch_shapes=[pltpu.VMEM((tm, D_MODEL), BF16)],
        compiler_params=pltpu.CompilerParams(
            dimension_semantics=("parallel", "arbitrary"), vmem_limit_bytes=VMEM_LIMIT),
        name="proj",
    )(x2, g, wm, ws, wst, wvt)


def _attn_body(q_ref, k_ref, vt_ref, z_ref, lqk_ref, subln_ref, o_ref, acc_scr, *, tq, tk, n_q, lam_init):
    h = pl.program_id(1)
    sub = LANES
    nsub = tk // sub
    nch = tk // tq
    nr = 2 * tq
    lane = lax.broadcasted_iota(jnp.int32, (tq, LANES), 1)
    scale = ATT_HD ** -0.5 * LOG2E
    slope_row = LOG2E * jnp.exp2(jnp.full((1, nr), -8.0 / ATT_HEADS, F32) * (h + 1).astype(F32))
    krow = lax.broadcasted_iota(jnp.int32, (sub, nr), 0)
    qcol = lax.broadcasted_iota(jnp.int32, (sub, nr), 1)
    lag = krow - jnp.where(qcol >= tq, qcol - tq, qcol)
    bias = slope_row * krow.astype(F32)
    ones_rows = jnp.ones((ATT_ONES, tk), BF16)
    lqk = lqk_ref[...]
    lam = (jnp.exp(jnp.sum(lqk[0:1] * lqk[1:2], axis=-1, keepdims=True))
           - jnp.exp(jnp.sum(lqk[2:3] * lqk[3:4], axis=-1, keepdims=True)) + lam_init)

    chains = range(nch)

    def load_queries(qb):
        qs = []
        for c in chains:
            q = q_ref[0, qb * tk + c * tq:qb * tk + (c + 1) * tq, :].astype(F32) * scale
            qs.append(jnp.concatenate([jnp.where(lane < ATT_HD, q, 0.0),
                                       jnp.where(lane >= ATT_HD, q, 0.0)], axis=0).astype(BF16))
        return qs

    def key_count(diag):
        return [c + 1 if diag else nsub for c in chains]

    def scores(qs, j, diag):
        ns = key_count(diag)
        return [lax.dot_general(k_ref[0, j * tk:j * tk + ns[c] * sub, :], qs[c], NT,
                                preferred_element_type=F32) for c in chains]

    def update(qb, j, s_all, ms, diag):
        ns = key_count(diag)
        vt = jnp.concatenate([vt_ref[0, :, j * tk:(j + 1) * tk], ones_rows], axis=0)
        s, shift, ms_new = [], [], []
        for c in chains:
            sc, shc = [], []
            m_new = ms[c]
            for t in range(ns[c]):
                st = s_all[c][t * sub:(t + 1) * sub] + bias
                if diag and t == c:
                    st = jnp.where(lag <= 0, st, -1e30)
                sh = slope_row * float((j - qb) * tk + (t - c) * sub)
                m_new = jnp.maximum(m_new, jnp.max(st, axis=0, keepdims=True) + sh)
                sc.append(st)
                shc.append(sh)
            s.append(sc)
            shift.append(shc)
            ms_new.append(m_new)
        p = [jnp.concatenate([jnp.exp2((s[c][t] - (ms_new[c] - shift[c][t])).astype(BF16))
                              for t in range(ns[c])], axis=0) for c in chains]
        pv = [jnp.dot(vt[:, :ns[c] * sub], p[c], preferred_element_type=F32) for c in chains]
        for c in chains:
            if j == 0:
                acc_scr[c] = pv[c]
            else:
                acc_scr[c] = jnp.exp2(ms[c] - ms_new[c]) * acc_scr[c] + pv[c]
        return ms_new

    def finish(qb):
        for c in chains:
            rows = slice(qb * tk + c * tq, qb * tk + (c + 1) * tq)
            acc = acc_scr[c]
            ot = acc[:ATT_VD] * (1.0 / acc[ATT_VD:ATT_VD + 1])
            o = (ot[:, :tq] - lam * ot[:, tq:]).T
            o = o * lax.rsqrt(jnp.mean(o * o, axis=-1, keepdims=True) + EPS) * subln_ref[...] * (1.0 - lam_init)
            z = z_ref[0, rows, :].astype(F32)
            o_ref[0, rows, :] = (o * (z * _sigmoid(z))).astype(BF16)

    steps = [(qb, j) for qb in range(n_q // nch) for j in range(qb + 1)]
    qs = load_queries(0)
    s_next = scores(qs, 0, True)
    ms = None
    for n, (qb, j) in enumerate(steps):
        s_cur = s_next
        if n + 1 < len(steps):
            qb2, j2 = steps[n + 1]
            if qb2 != qb:
                qs = load_queries(qb2)
            s_next = scores(qs, j2, j2 == qb2)
        if j == 0:
            ms = [jnp.full((1, nr), -1e30, F32) for _ in chains]
        ms = update(qb, j, s_cur, ms, j == qb)
        if j == qb:
            finish(qb)


def _attn(pm3, vt, lqk, subln, *, tq, lam_init):
    b, s, _ = pm3.shape
    nq = ATT_W // LANES
    seq = lambda col: pl.BlockSpec((1, s, LANES), lambda bi, h: (bi, 0, col + h))
    tk = _pick(s, (512, 256, 128))
    return pl.pallas_call(
        functools.partial(_attn_body, tq=tq, tk=tk, n_q=s // tq, lam_init=lam_init),
        grid=(b, ATT_HEADS),
        in_specs=[
            seq(0), seq(nq),
            pl.BlockSpec((1, LANES, s), lambda bi, h: (bi, h, 0)),
            seq(2 * nq),
            pl.BlockSpec((4, ATT_HD), lambda bi, h: (0, 0)),
            pl.BlockSpec((1, ATT_VD), lambda bi, h: (0, 0)),
        ],
        out_specs=pl.BlockSpec((1, s, LANES), lambda bi, h: (bi, 0, h)),
        out_shape=jax.ShapeDtypeStruct((b, s, ATT_W), BF16),
        scratch_shapes=[pltpu.VMEM((tk // tq, ATT_VD + ATT_ONES, 2 * tq), F32)],
        compiler_params=pltpu.CompilerParams(
            dimension_semantics=("parallel", "parallel"), vmem_limit_bytes=VMEM_LIMIT),
        name="attn",
    )(pm3, pm3, vt, pm3, lqk, subln)


DN_GROUP = 2
DN_SUPER = 2


def _dn_body(dq_ref, dk_ref, dv_ref, dz_ref, ps_ref, rows_ref, cw_ref, acol_ref, dtcol_ref,
             arow_ref, dtrow_ref, norm_ref, ltri_ref, utri_ref, shift_ref, o_ref,
             st_scr, u_scr, wq_scr, kd_scr, a_scr, eg_scr, *, n_chunks):
    c_ = CHUNK
    gr = DN_GROUP * c_
    ii = lax.broadcasted_iota(jnp.int32, (c_, c_), 0)
    jj = lax.broadcasted_iota(jnp.int32, (c_, c_), 1)
    tri_incl = ii >= jj
    tri_strict = ii > jj
    eye = ii == jj
    cw = cw_ref[...]
    heads = range(DN_HEADS)
    lanes = [slice(h * DN_HD, (h + 1) * DN_HD) for h in heads]

    def conv_silu(win, w4):
        delayed = jnp.dot(shift_ref[...], win, preferred_element_type=F32)
        y = (delayed[0:gr] * w4[0:1] + delayed[gr:2 * gr] * w4[1:2] + delayed[2 * gr:3 * gr] * w4[2:3]
             + win[gr:2 * gr].astype(F32) * w4[3:4])
        return y * _sigmoid(y)

    def window(ref, r0, first):
        if first:
            return jnp.concatenate([jnp.zeros((gr, DN_W), BF16), ref[0, 0:gr, :]], axis=0)
        return ref[0, pl.ds(pl.multiple_of(r0 - gr, gr), 2 * gr), :]

    def mm(a, b):
        return jnp.dot(a.astype(BF16), b.astype(BF16), preferred_element_type=F32)

    def gather(gi, first):
        r0 = gi * gr if isinstance(gi, int) else pl.multiple_of(gi * gr, gr)
        cq = conv_silu(window(dq_ref, r0, first), cw[:, 0:DN_W])
        ck = conv_silu(window(dk_ref, r0, first), cw[:, DN_W:2 * DN_W])
        cv = conv_silu(window(dv_ref, r0, first), cw[:, 2 * DN_W:3 * DN_W])
        ps = ps_ref[pl.ds(r0, gr), :]
        beta_all = _sigmoid(ps)
        g_all = -jnp.exp(acol_ref[...]) * _softplus(ps + dtcol_ref[...])
        g_hi, g_lo = _split_bf16(g_all)
        chains = []
        for j in range(DN_GROUP):
            cidx = gi * DN_GROUP + j
            rs = slice(j * c_, (j + 1) * c_)
            gc_all = jnp.dot(ltri_ref[...], jnp.concatenate([g_hi[rs], g_lo[rs]], axis=0),
                             preferred_element_type=F32)
            g_row = -jnp.exp(arow_ref[...]) * _softplus(rows_ref[0, cidx] + dtrow_ref[...])
            r_hi, r_lo = _split_bf16(g_row)
            gc_rows = jnp.dot(jnp.concatenate([r_hi, r_lo], axis=1), utri_ref[...],
                              preferred_element_type=F32)
            for h in heads:
                qh, kh, vh = cq[rs, lanes[h]], ck[rs, lanes[h]], cv[rs, lanes[h]]
                qn = qh * (lax.rsqrt(jnp.sum(qh * qh, axis=-1, keepdims=True) + EPS) * DN_HD ** -0.5)
                kn = kh * lax.rsqrt(jnp.sum(kh * kh, axis=-1, keepdims=True) + EPS)
                beta = beta_all[rs, h:h + 1]
                gcc = gc_all[:, 4 + h:5 + h]
                gcr = gc_rows[4 + h:5 + h, :]
                gl = gcr[:, c_ - 1:c_]
                chains.append(dict(
                    cidx=cidx, h=h, rows=pl.ds(r0 + j * c_, c_), qn=qn, kn=kn, kb=kn * beta, vb=vh * beta,
                    gcc=gcc, gl=gl, decay=jnp.exp(jnp.where(tri_incl, gcc - gcr, -1e30))))
        return chains

    def prep(si, first):
        chains = []
        for sg in range(DN_SUPER):
            chains += gather(si * DN_SUPER + sg, first and sg == 0)
        for ch in chains:
            ch["gram"] = lax.dot_general(jnp.concatenate([ch["kb"], ch["qn"]], axis=0).astype(BF16),
                                         ch["kn"].astype(BF16), NT, preferred_element_type=F32)
        for ch in chains:
            x = jnp.where(tri_strict, -(ch["gram"][:c_] * ch["decay"]), 0.0)
            a_scr[ch["cidx"], ch["h"]] = (ch["gram"][c_:] * ch["decay"]).astype(BF16)
            ch["acc"] = jnp.where(eye, 1.0, x)
            ch["x"] = x
        for ch in chains:
            ch["xp"] = mm(ch["x"], ch["x"])
        for _ in range(4):
            for ch in chains:
                ch["both"] = mm(jnp.concatenate([ch["acc"], ch["xp"]], axis=0), ch["xp"])
            for ch in chains:
                ch["acc"] = ch["acc"] + ch["both"][:c_]
                ch["xp"] = ch["both"][c_:]
        for ch in chains:
            ch["last"] = mm(ch["acc"], ch["xp"])
        for ch in chains:
            t_inv = ch["acc"] + ch["last"]
            gexp = jnp.exp(ch["gcc"])
            ch["gexp"] = gexp
            ch["uw"] = mm(t_inv, jnp.concatenate([ch["vb"], ch["kb"] * gexp], axis=1))
        for ch in chains:
            cidx, h, rows = ch["cidx"], ch["h"], ch["rows"]
            u_scr[rows, lanes[h]] = ch["uw"][:, :DN_HD]
            wq_scr[cidx, 0:c_, lanes[h]] = ch["uw"][:, DN_HD:].astype(BF16)
            wq_scr[cidx, c_:2 * c_, lanes[h]] = (ch["qn"] * ch["gexp"]).astype(BF16)
            kd_scr[rows, lanes[h]] = (ch["kn"] * jnp.exp(ch["gl"] - ch["gcc"])).astype(BF16)
            eg_scr[cidx, h] = jnp.broadcast_to(jnp.exp(ch["gl"]), (8, DN_HD))

    prep(0, True)

    def prep_body(gi, carry):
        prep(gi, False)
        return carry

    lax.fori_loop(1, n_chunks // (DN_GROUP * DN_SUPER), prep_body, 0)

    st_scr[...] = jnp.zeros_like(st_scr)

    def scan(ci, carry):
        states = [st_scr[h] for h in heads]
        for u in range(DN_GROUP):
            c = ci * DN_GROUP + u
            rows = pl.ds(pl.multiple_of(c * c_, c_), c_)
            wq = [jnp.dot(wq_scr[c, :, lanes[h]], states[h].astype(BF16), preferred_element_type=F32)
                  for h in heads]
            v_new = [(u_scr[rows, lanes[h]] - wq[h][:c_]).astype(BF16) for h in heads]
            kv = [lax.dot_general(kd_scr[rows, lanes[h]], v_new[h], TN, preferred_element_type=F32)
                  for h in heads]
            o = [wq[h][c_:] + jnp.dot(a_scr[c, h], v_new[h], preferred_element_type=F32) for h in heads]
            states = [states[h] * eg_scr[c, h][0:1, :] + kv[h] for h in heads]
            for h in heads:
                on = o[h] * lax.rsqrt(jnp.mean(o[h] * o[h], axis=-1, keepdims=True) + EPS) * norm_ref[...]
                z = dz_ref[0, rows, lanes[h]].astype(F32)
                o_ref[0, rows, lanes[h]] = (on * (z * _sigmoid(z))).astype(BF16)
        for h in heads:
            st_scr[h] = states[h]
        return carry

    lax.fori_loop(0, n_chunks // DN_GROUP, scan, 0)


def _deltanet(pm3, ps, rows4, cw, acol, dtcol, arow, dtrow, norm, ltri, utri, shift):
    b, s, _ = pm3.shape
    n_chunks = s // CHUNK
    assert n_chunks % (DN_GROUP * DN_SUPER) == 0
    base = 3 * ATT_W // DN_W
    full = lambda shape: pl.BlockSpec(shape, lambda bi: (0,) * len(shape))
    return pl.pallas_call(
        functools.partial(_dn_body, n_chunks=n_chunks),
        grid=(b,),
        in_specs=[
            pl.BlockSpec((1, s, DN_W), lambda bi: (bi, 0, base)),
            pl.BlockSpec((1, s, DN_W), lambda bi: (bi, 0, base + 1)),
            pl.BlockSpec((1, s, DN_W), lambda bi: (bi, 0, base + 2)),
            pl.BlockSpec((1, s, DN_W), lambda bi: (bi, 0, base + 3)),
            pl.BlockSpec((s, D_SMALL), lambda bi: (bi, 0)),
            pl.BlockSpec((1, n_chunks, 8, CHUNK), lambda bi: (bi, 0, 0, 0)),
            full((DN_CONV, 3 * DN_W)),
            full((1, D_SMALL)), full((1, D_SMALL)), full((8, 1)), full((8, 1)),
            full((1, DN_HD)),
            full((CHUNK, 2 * CHUNK)), full((2 * CHUNK, CHUNK)), full(shift.shape),
        ],
        out_specs=pl.BlockSpec((1, s, DN_W), lambda bi: (bi, 0, 0)),
        out_shape=jax.ShapeDtypeStruct((b, s, DN_W), BF16),
        scratch_shapes=[
            pltpu.VMEM((DN_HEADS, DN_HD, DN_HD), F32),
            pltpu.VMEM((s, DN_W), F32),
            pltpu.VMEM((n_chunks, 2 * CHUNK, DN_W), BF16),
            pltpu.VMEM((s, DN_W), BF16),
            pltpu.VMEM((n_chunks, DN_HEADS, CHUNK, CHUNK), BF16),
            pltpu.VMEM((n_chunks, DN_HEADS, 8, DN_HD), F32),
        ],
        compiler_params=pltpu.CompilerParams(
            dimension_semantics=("parallel",), vmem_limit_bytes=VMEM_LIMIT),
        name="deltanet",
    )(pm3, pm3, pm3, pm3, ps, rows4, cw, acol, dtcol, arow, dtrow, norm, ltri, utri, shift)


GLA_GROUP = 2
GLA_PREP = 4


def _gla_body(gq_ref, gk_ref, gv_ref, gz_ref, ps_ref, w2_ref, b2_ref, norm_ref, cmat_ref,
              o_ref, st_scr, att_scr, qe_scr, ke_scr, dec_scr, *, n_chunks):
    c_ = CHUNK
    rowi = lax.broadcasted_iota(jnp.int32, (c_, 1), 0)
    upper = [(rowi % (2 * s)) >= s for s in GLA_LEVELS]
    ii = lax.broadcasted_iota(jnp.int32, (2 * c_, c_), 0) % c_
    jj = lax.broadcasted_iota(jnp.int32, (2 * c_, c_), 1)
    same = [(ii // (2 * s)) == (jj // (2 * s)) for s in GLA_LEVELS]
    eye = ii == jj
    lane = lax.broadcasted_iota(jnp.int32, (c_, LANES), 1)
    first_head = lane < GLA_KD
    lane_sq = lax.broadcasted_iota(jnp.int32, (GLA_VD, LANES), 1)
    head_lanes = [lane_sq < GLA_KD, lane_sq >= GLA_KD]

    def stack_heads(t):
        return jnp.concatenate([jnp.where(first_head, t, 0.0), jnp.where(first_head, 0.0, t)], axis=0)

    pairs = range(GLA_HEADS // 2)
    pair_lanes = [slice(p * LANES, (p + 1) * LANES) for p in pairs]
    gr = GLA_PREP * c_

    def prep(gi, carry):
        r0 = pl.multiple_of(gi * gr, gr)
        pre = (jnp.dot(ps_ref[pl.ds(r0, gr), :].astype(BF16), w2_ref[...], preferred_element_type=F32)
               + b2_ref[...])
        gk = (jnp.minimum(pre, 0.0) - jnp.log(1.0 + jnp.exp(-jnp.abs(pre)))) * (1.0 / GLA_TAU)
        g_hi, g_lo = _split_bf16(gk)
        chains = []
        for j in range(GLA_PREP):
            rs = slice(j * c_, (j + 1) * c_)
            rows = pl.ds(r0 + j * c_, c_)
            br = jnp.dot(cmat_ref[...], jnp.concatenate([g_hi[rs], g_lo[rs]], axis=0),
                         preferred_element_type=F32)
            for p in pairs:
                q = gq_ref[0, rows, pair_lanes[p]].astype(F32) * (GLA_KD ** -0.5)
                k = gk_ref[0, rows, pair_lanes[p]].astype(F32)
                chains.append(dict(cidx=gi * GLA_PREP + j, p=p, rows=rows, br=br, q=q, k=k,
                                   bcum=br[0:c_, pair_lanes[p]]))
        for ch in chains:
            ch["part"] = lax.dot_general(stack_heads(ch["q"]).astype(BF16), ch["k"].astype(BF16), NT,
                                         preferred_element_type=F32)
        for ch in chains:
            ch["att"] = jnp.where(eye, ch["part"], 0.0)
        for li in range(len(GLA_LEVELS)):
            up = upper[li]
            for ch in chains:
                ref = ch["br"][c_ * (li + 1):c_ * (li + 2), pair_lanes[ch["p"]]]
                e = jnp.exp(jnp.where(up, ch["bcum"] - ref, ref - ch["bcum"]))
                qt = jnp.where(up, ch["q"] * e, 0.0)
                kt = jnp.where(up, 0.0, ch["k"] * e)
                ch["part"] = lax.dot_general(stack_heads(qt).astype(BF16), kt.astype(BF16), NT,
                                             preferred_element_type=F32)
            for ch in chains:
                ch["att"] = ch["att"] + jnp.where(same[li], ch["part"], 0.0)
        for ch in chains:
            cidx, p, rows, bcum = ch["cidx"], ch["p"], ch["rows"], ch["bcum"]
            blast = bcum[c_ - 1:c_, :]
            att_scr[cidx, 2 * p] = ch["att"][:c_].astype(BF16)
            att_scr[cidx, 2 * p + 1] = ch["att"][c_:].astype(BF16)
            qe_scr[rows, pair_lanes[p]] = (ch["q"] * jnp.exp(bcum)).astype(BF16)
            ke_scr[rows, pair_lanes[p]] = (ch["k"] * jnp.exp(blast - bcum)).astype(BF16)
            dec_scr[cidx, p] = jnp.broadcast_to(jnp.exp(blast), (8, LANES))
        return carry

    lax.fori_loop(0, n_chunks // GLA_PREP, prep, 0)

    st_scr[...] = jnp.zeros_like(st_scr)
    heads = range(GLA_HEADS)
    head_cols = [slice(h * GLA_VD, (h + 1) * GLA_VD) for h in heads]

    def scan(ci, carry):
        states = [st_scr[h] for h in heads]
        for u in range(GLA_GROUP):
            c = ci * GLA_GROUP + u
            rows = pl.ds(pl.multiple_of(c * c_, c_), c_)
            qe = [qe_scr[rows, pair_lanes[p]] for p in pairs]
            ke = [ke_scr[rows, pair_lanes[p]] for p in pairs]
            vh = [gv_ref[0, rows, head_cols[h]] for h in heads]
            upd = [lax.dot_general(vh[h], ke[h // 2], TN, preferred_element_type=F32) for h in heads]
            intra = [jnp.dot(att_scr[c, h], vh[h], preferred_element_type=F32) for h in heads]
            inter = [lax.dot_general(qe[h // 2], states[h].astype(BF16), NT, preferred_element_type=F32)
                     for h in heads]
            states = [states[h] * dec_scr[c, h // 2][0:1, :] + jnp.where(head_lanes[h % 2], upd[h], 0.0)
                      for h in heads]
            for h in heads:
                o = inter[h] + intra[h]
                o = o * lax.rsqrt(jnp.mean(o * o, axis=-1, keepdims=True) + EPS) * norm_ref[...]
                z = gz_ref[0, rows, head_cols[h]].astype(F32)
                o_ref[0, rows, head_cols[h]] = (o * (z * _sigmoid(z))).astype(BF16)
        for h in heads:
            st_scr[h] = states[h]
        return carry

    lax.fori_loop(0, n_chunks // GLA_GROUP, scan, 0)


def _gla(pm3, ps, w2, b2, norm, cmat):
    b, s, _ = pm3.shape
    n_chunks = s // CHUNK
    off = 3 * ATT_W + 4 * DN_W
    full = lambda shape: pl.BlockSpec(shape, lambda bi: (0,) * len(shape))
    return pl.pallas_call(
        functools.partial(_gla_body, n_chunks=n_chunks),
        grid=(b,),
        in_specs=[
            pl.BlockSpec((1, s, GLA_KW), lambda bi: (bi, 0, off // GLA_KW)),
            pl.BlockSpec((1, s, GLA_KW), lambda bi: (bi, 0, off // GLA_KW + 1)),
            pl.BlockSpec((1, s, GLA_W), lambda bi: (bi, 0, (off + 2 * GLA_KW) // GLA_W)),
            pl.BlockSpec((1, s, GLA_W), lambda bi: (bi, 0, (off + 2 * GLA_KW) // GLA_W + 1)),
            pl.BlockSpec((s, D_SMALL), lambda bi: (bi, 0)),
            full((D_SMALL, GLA_KW)), full((1, GLA_KW)), full((1, GLA_VD)),
            full(((len(GLA_LEVELS) + 1) * CHUNK, 2 * CHUNK)),
        ],
        out_specs=pl.BlockSpec((1, s, GLA_W), lambda bi: (bi, 0, 0)),
        out_shape=jax.ShapeDtypeStruct((b, s, GLA_W), BF16),
        scratch_shapes=[
            pltpu.VMEM((GLA_HEADS, GLA_VD, LANES), F32),
            pltpu.VMEM((n_chunks, GLA_HEADS, CHUNK, CHUNK), BF16),
            pltpu.VMEM((s, GLA_KW), BF16),
            pltpu.VMEM((s, GLA_KW), BF16),
            pltpu.VMEM((n_chunks, GLA_HEADS // 2, 8, LANES), F32),
        ],
        compiler_params=pltpu.CompilerParams(
            dimension_semantics=("parallel",), vmem_limit_bytes=VMEM_LIMIT),
        name="gla",
    )(pm3, pm3, pm3, pm3, ps, w2, b2, norm, cmat)


def _out_body(x_ref, ya_ref, yd_ref, yg_ref, p_ref, wo_ref, postg_ref, pg_ref, pp_ref, pn_ref, o_ref):
    y = (jnp.dot(ya_ref[...], wo_ref[0:ATT_W, :], preferred_element_type=F32)
         + jnp.dot(yd_ref[...], wo_ref[ATT_W:ATT_W + DN_W, :], preferred_element_type=F32)
         + jnp.dot(yg_ref[...], wo_ref[ATT_W + DN_W:D_MIX, :], preferred_element_type=F32))
    x1 = x_ref[...] + y * lax.rsqrt(jnp.mean(y * y, axis=-1, keepdims=True) + EPS) * postg_ref[...]
    gate = _sigmoid(jnp.dot(x1.astype(BF16), pg_ref[...], preferred_element_type=F32))
    pe = jnp.dot(p_ref[...].astype(BF16), pp_ref[...], preferred_element_type=F32) * gate
    o_ref[...] = x1 + pe * lax.rsqrt(jnp.mean(pe * pe, axis=-1, keepdims=True) + EPS) * pn_ref[...]


def _out(x2, ya, yd, yg, p3, wo, postg, pg, pp, pn, *, layer, tm):
    t = x2.shape[0]
    tile = lambda w: pl.BlockSpec((tm, w), lambda i: (i, 0))
    full = lambda shape: pl.BlockSpec(shape, lambda i: (0, 0))
    stacked = lambda shape: pl.BlockSpec((None,) + shape, lambda i: (layer, 0, 0))
    return pl.pallas_call(
        _out_body,
        grid=(t // tm,),
        in_specs=[tile(D_MODEL), tile(ATT_W), tile(DN_W), tile(GLA_W),
                  pl.BlockSpec((None, tm, D_PLE), lambda i: (layer, i, 0)),
                  stacked((D_MIX, D_MODEL)), full((1, D_MODEL)), stacked((D_MODEL, D_MODEL)),
                  stacked((D_PLE, D_MODEL)), full((1, D_MODEL))],
        out_specs=tile(D_MODEL),
        out_shape=jax.ShapeDtypeStruct((t, D_MODEL), F32),
        compiler_params=pltpu.CompilerParams(
            dimension_semantics=("parallel",), vmem_limit_bytes=VMEM_LIMIT),
        name="out",
    )(x2, ya, yd, yg, p3, wo, postg, pg, pp, pn)


def _chunk_constants():
    c = CHUNK
    ltri = np.tril(np.ones((c, c), np.float32))
    blocks = [ltri]
    idx = np.arange(c)
    for s in GLA_LEVELS:
        bnd = (idx // (2 * s)) * (2 * s) + s - 1
        blocks.append(ltri[bnd])
    cmat = np.concatenate(blocks, axis=0)
    gr = DN_GROUP * c
    shift = np.zeros((DN_CONV - 1, gr, 2 * gr), np.float32)
    t = np.arange(gr)
    for i in range(DN_CONV - 1):
        shift[i, t, gr + t - (DN_CONV - 1) + i] = 1.0
    return (jnp.asarray(np.concatenate([ltri, ltri], axis=1), BF16),
            jnp.asarray(np.concatenate([ltri.T, ltri.T], axis=0), BF16),
            jnp.asarray(np.concatenate([cmat, cmat], axis=1), BF16),
            jnp.asarray(shift.reshape((DN_CONV - 1) * gr, 2 * gr), BF16))


def _pick(n, pref):
    for t in pref:
        if n % t == 0:
            return t
    return n


def _prepare_weights(w_in, w_out, ple_proj, ple_gate):
    n_big = 4 * ATT_W + 4 * DN_W
    n_small = 2 * DN_HEADS
    wt = jnp.swapaxes(w_in, 1, 2)
    wm = jnp.concatenate([wt[:, :2 * ATT_W], wt[:, 3 * ATT_W:n_big],
                          wt[:, n_big + n_small:n_big + n_small + 2 * GLA_KW + 2 * GLA_W]],
                         axis=1).astype(BF16)
    wvt = wt[:, 2 * ATT_W:3 * ATT_W].astype(BF16)
    wst = wt[:, n_big:n_big + n_small].astype(BF16)
    w_small = jnp.concatenate([w_in[:, :, n_big:n_big + n_small], w_in[:, :, -GLA_RANK:]], axis=2)
    ws = jnp.pad(w_small, ((0, 0), (0, 0), (0, D_SMALL - w_small.shape[2]))).astype(BF16)
    return wm, ws, wst, wvt, w_out.astype(BF16), ple_gate.astype(BF16), ple_proj.astype(BF16)


def _layer(x, p3, weights, pre_g, post_g, lq1, lk1, lq2, lk2, att_subln, dn_conv, dn_a_log,
           dn_dt_bias, dn_norm, gla_w2, gla_b, gla_norm, ple_norm, layer_idx, consts):
    b, s, _ = x.shape
    t = b * s
    ltri, utri, cmat, shift = consts
    wm, ws, wst, wvt, wo, pg, pp = weights
    lam_init = 0.8 - 0.6 * math.exp(-0.3 * layer_idx)
    n_small = 2 * DN_HEADS

    x2 = x.reshape(t, D_MODEL)
    pm, ps, pst, vt = _proj(x2, pre_g.reshape(1, D_MODEL), wm, ws, wst, wvt, layer=layer_idx, b=b, s=s,
                            tm=_pick(s, (1024, 512, 256)), tn=_pick(D_MAIN, (3328, 512)))
    pm3 = pm.reshape(b, s, D_MAIN)

    lqk = jnp.stack([lq1, lk1, lq2, lk2], axis=0)
    y_att = _attn(pm3, vt, lqk, att_subln.reshape(1, ATT_VD), tq=LANES, lam_init=lam_init)

    rows4 = pst.reshape(8, b, s // CHUNK, CHUNK).transpose(1, 2, 0, 3)
    lane_vec = lambda v, off: jnp.zeros((1, D_SMALL), F32).at[0, off:off + v.shape[0]].set(v)
    row_vec = lambda v, off: jnp.zeros((8, 1), F32).at[off:off + v.shape[0], 0].set(v)
    y_dn = _deltanet(pm3, ps, rows4, dn_conv, lane_vec(dn_a_log, DN_HEADS), lane_vec(dn_dt_bias, DN_HEADS),
                     row_vec(dn_a_log, DN_HEADS), row_vec(dn_dt_bias, DN_HEADS),
                     dn_norm.reshape(1, DN_HD), ltri, utri, shift)

    w2 = jnp.zeros((D_SMALL, GLA_KW), F32).at[n_small:n_small + GLA_RANK].set(gla_w2).astype(BF16)
    y_gla = _gla(pm3, ps, w2, gla_b.reshape(1, GLA_KW), gla_norm.reshape(1, GLA_VD), cmat)

    out = _out(x2, y_att.reshape(t, ATT_W), y_dn.reshape(t, DN_W), y_gla.reshape(t, GLA_W), p3,
               wo, post_g.reshape(1, D_MODEL), pg, pp, ple_norm.reshape(1, D_MODEL),
               layer=layer_idx, tm=_pick(t, (512, 256)))
    return out.reshape(b, s, D_MODEL)


def kernel(x, p, w_in, w_out, pre_gain, post_gain, att_lq1, att_lk1, att_lq2, att_lk2, att_subln, dn_conv, dn_a_log, dn_dt_bias, dn_norm, gla_w2, gla_b, gla_norm, ple_proj, ple_gate, ple_norm):
    consts = _chunk_constants()
    weights = _prepare_weights(w_in, w_out, ple_proj, ple_gate)
    p3 = p.reshape(p.shape[0], -1, D_PLE)
    for i in range(p.shape[0]):
        x = _layer(x, p3, weights, pre_gain[i], post_gain[i], att_lq1[i], att_lk1[i],
                   att_lq2[i], att_lk2[i], att_subln[i], dn_conv[i], dn_a_log[i], dn_dt_bias[i],
                   dn_norm[i], gla_w2[i], gla_b[i], gla_norm[i], ple_norm[i], i, consts)
    return x
```

```python
import functools
import math

import numpy as np
import jax
import jax.numpy as jnp
from jax import lax
from jax.experimental import pallas as pl
from jax.experimental.pallas import tpu as pltpu

F32 = jnp.float32
BF16 = jnp.bfloat16

D_MODEL = 1024
D_PLE = 256
EPS = 1e-6
ATT_HEADS = 8
ATT_HD = 64
ATT_VD = 128
ATT_W = 1024
DN_HEADS = 4
DN_HD = 128
DN_W = 512
DN_CONV = 4
GLA_HEADS = 4
GLA_KD = 64
GLA_VD = 128
GLA_KW = 256
GLA_W = 512
GLA_RANK = 16
GLA_TAU = 16.0
CHUNK = 64
D_MIX = 2048
D_MAIN = 6656
D_SMALL = 128
LANES = 128
VMEM_LIMIT = 48 * 1024 * 1024

NT = (((1,), (1,)), ((), ()))
TN = (((0,), (0,)), ((), ()))
GLA_LEVELS = (32, 16, 8, 4, 2, 1)
LOG2E = math.log2(math.e)
ATT_ONES = 16


def _sigmoid(x):
    return 1.0 / (1.0 + jnp.exp(-x))


def _softplus(x):
    return jnp.maximum(x, 0.0) + jnp.log(1.0 + jnp.exp(-jnp.abs(x)))


def _split_bf16(x):
    hi = x.astype(BF16)
    lo = (x - hi.astype(F32)).astype(BF16)
    return hi, lo


def _proj_body(x_ref, g_ref, wm_ref, ws_ref, wst_ref, wvt_ref, pm_ref, ps_ref, pst_ref, vt_ref, h_scr):
    @pl.when(pl.program_id(1) == 0)
    def _():
        x = x_ref[...]
        h = (x * lax.rsqrt(jnp.mean(x * x, axis=-1, keepdims=True) + EPS) * g_ref[...]).astype(BF16)
        h_scr[...] = h
        ps_ref[...] = jnp.dot(h, ws_ref[...], preferred_element_type=F32)
        pst_ref[...] = lax.dot_general(wst_ref[...], h, NT, preferred_element_type=F32)
        vt_ref[0] = lax.dot_general(wvt_ref[...], h, NT, preferred_element_type=F32).astype(BF16)

    pm_ref[...] = lax.dot_general(h_scr[...], wm_ref[...], NT, preferred_element_type=F32).astype(BF16)


def _proj(x2, g, wm, ws, wst, wvt, *, layer, b, s, tm, tn):
    t = x2.shape[0]
    spb = s // tm
    return pl.pallas_call(
        _proj_body,
        grid=(t // tm, D_MAIN // tn),
        in_specs=[
            pl.BlockSpec((tm, D_MODEL), lambda i, j: (i, 0)),
            pl.BlockSpec((1, D_MODEL), lambda i, j: (0, 0)),
            pl.BlockSpec((None, tn, D_MODEL), lambda i, j: (layer, j, 0)),
            pl.BlockSpec((None, D_MODEL, D_SMALL), lambda i, j: (layer, 0, 0)),
            pl.BlockSpec((None, 8, D_MODEL), lambda i, j: (layer, 0, 0)),
            pl.BlockSpec((None, ATT_W, D_MODEL), lambda i, j: (layer, 0, 0)),
        ],
        out_specs=[
            pl.BlockSpec((tm, tn), lambda i, j: (i, j)),
            pl.BlockSpec((tm, D_SMALL), lambda i, j: (i, 0)),
            pl.BlockSpec((8, tm), lambda i, j: (0, i)),
            pl.BlockSpec((1, ATT_W, tm), lambda i, j: (i // spb, 0, i % spb)),
        ],
        out_shape=[
            jax.ShapeDtypeStruct((t, D_MAIN), BF16),
            jax.ShapeDtypeStruct((t, D_SMALL), F32),
            jax.ShapeDtypeStruct((8, t), F32),
            jax.ShapeDtypeStruct((b, ATT_W, s), BF16),
        ],
        scratch_shapes=[pltpu.VMEM((tm, D_MODEL), BF16)],
        compiler_params=pltpu.CompilerParams(
            dimension_semantics=("parallel", "arbitrary"), vmem_limit_bytes=VMEM_LIMIT),
        name="proj",
    )(x2, g, wm, ws, wst, wvt)


def _attn_body(q_ref, k_ref, vt_ref, z_ref, lqk_ref, subln_ref, o_ref, acc_scr, *, tq, tk, n_q, lam_init):
    h = pl.program_id(1)
    sub = LANES
    nsub = tk // sub
    nch = tk // tq
    nr = 2 * tq
    lane = lax.broadcasted_iota(jnp.int32, (tq, LANES), 1)
    scale = ATT_HD ** -0.5 * LOG2E
    slope_row = LOG2E * jnp.exp2(jnp.full((1, nr), -8.0 / ATT_HEADS, F32) * (h + 1).astype(F32))
    krow = lax.broadcasted_iota(jnp.int32, (sub, nr), 0)
    qcol = lax.broadcasted_iota(jnp.int32, (sub, nr), 1)
    lag = krow - jnp.where(qcol >= tq, qcol - tq, qcol)
    bias = slope_row * krow.astype(F32)
    ones_rows = jnp.ones((ATT_ONES, tk), BF16)
    lqk = lqk_ref[...]
    lam = (jnp.exp(jnp.sum(lqk[0:1] * lqk[1:2], axis=-1, keepdims=True))
           - jnp.exp(jnp.sum(lqk[2:3] * lqk[3:4], axis=-1, keepdims=True)) + lam_init)

    chains = range(nch)

    def load_queries(qb):
        qs = []
        for c in chains:
            q = q_ref[0, qb * tk + c * tq:qb * tk + (c + 1) * tq, :].astype(F32) * scale
            qs.append(jnp.concatenate([jnp.where(lane < ATT_HD, q, 0.0),
                                       jnp.where(lane >= ATT_HD, q, 0.0)], axis=0).astype(BF16))
        return qs

    def key_count(diag):
        return [c + 1 if diag else nsub for c in chains]

    def scores(qs, j, diag):
        ns = key_count(diag)
        return [lax.dot_general(k_ref[0, j * tk:j * tk + ns[c] * sub, :], qs[c], NT,
                                preferred_element_type=F32) for c in chains]

    def update(qb, j, s_all, ms, diag):
        ns = key_count(diag)
        vt = jnp.concatenate([vt_ref[0, :, j * tk:(j + 1) * tk], ones_rows], axis=0)
        s, shift, ms_new = [], [], []
        for c in chains:
            sc, shc = [], []
            m_new = ms[c]
            for t in range(ns[c]):
                st = s_all[c][t * sub:(t + 1) * sub] + bias
                if diag and t == c:
                    st = jnp.where(lag <= 0, st, -1e30)
                sh = slope_row * float((j - qb) * tk + (t - c) * sub)
                m_new = jnp.maximum(m_new, jnp.max(st, axis=0, keepdims=True) + sh)
                sc.append(st)
                shc.append(sh)
            s.append(sc)
            shift.append(shc)
            ms_new.append(m_new)
        p = [jnp.concatenate([jnp.exp2((s[c][t] - (ms_new[c] - shift[c][t])).astype(BF16))
                              for t in range(ns[c])], axis=0) for c in chains]
        pv = [jnp.dot(vt[:, :ns[c] * sub], p[c], preferred_element_type=F32) for c in chains]
        for c in chains:
            if j == 0:
                acc_scr[c] = pv[c]
            else:
                acc_scr[c] = jnp.exp2(ms[c] - ms_new[c]) * acc_scr[c] + pv[c]
        return ms_new

    def finish(qb):
        for c in chains:
            rows = slice(qb * tk + c * tq, qb * tk + (c + 1) * tq)
            acc = acc_scr[c]
            ot = acc[:ATT_VD] * (1.0 / acc[ATT_VD:ATT_VD + 1])
            o = (ot[:, :tq] - lam * ot[:, tq:]).T
            o = o * lax.rsqrt(jnp.mean(o * o, axis=-1, keepdims=True) + EPS) * subln_ref[...] * (1.0 - lam_init)
            z = z_ref[0, rows, :].astype(F32)
            o_ref[0, rows, :] = (o * (z * _sigmoid(z))).astype(BF16)

    steps = [(qb, j) for qb in range(n_q // nch) for j in range(qb + 1)]
    qs = load_queries(0)
    s_next = scores(qs, 0, True)
    ms = None
    for n, (qb, j) in enumerate(steps):
        s_cur = s_next
        if n + 1 < len(steps):
            qb2, j2 = steps[n + 1]
            if qb2 != qb:
                qs = load_queries(qb2)
            s_next = scores(qs, j2, j2 == qb2)
        if j == 0:
            ms = [jnp.full((1, nr), -1e30, F32) for _ in chains]
        ms = update(qb, j, s_cur, ms, j == qb)
        if j == qb:
            finish(qb)


def _attn(pm3, vt, lqk, subln, *, tq, lam_init):
    b, s, _ = pm3.shape
    nq = ATT_W // LANES
    seq = lambda col: pl.BlockSpec((1, s, LANES), lambda bi, h: (bi, 0, col + h))
    tk = _pick(s, (512, 256, 128))
    return pl.pallas_call(
        functools.partial(_attn_body, tq=tq, tk=tk, n_q=s // tq, lam_init=lam_init),
        grid=(b, ATT_HEADS),
        in_specs=[
            seq(0), seq(nq),
            pl.BlockSpec((1, LANES, s), lambda bi, h: (bi, h, 0)),
            seq(2 * nq),
            pl.BlockSpec((4, ATT_HD), lambda bi, h: (0, 0)),
            pl.BlockSpec((1, ATT_VD), lambda bi, h: (0, 0)),
        ],
        out_specs=pl.BlockSpec((1, s, LANES), lambda bi, h: (bi, 0, h)),
        out_shape=jax.ShapeDtypeStruct((b, s, ATT_W), BF16),
        scratch_shapes=[pltpu.VMEM((tk // tq, ATT_VD + ATT_ONES, 2 * tq), F32)],
        compiler_params=pltpu.CompilerParams(
            dimension_semantics=("parallel", "parallel"), vmem_limit_bytes=VMEM_LIMIT),
        name="attn",
    )(pm3, pm3, vt, pm3, lqk, subln)


DN_GROUP = 2
DN_SUPER = 2


def _dn_body(dq_ref, dk_ref, dv_ref, dz_ref, ps_ref, rows_ref, cw_ref, acol_ref, dtcol_ref,
             arow_ref, dtrow_ref, norm_ref, ltri_ref, utri_ref, shift_ref, o_ref,
             st_scr, u_scr, wq_scr, kd_scr, a_scr, eg_scr, *, n_chunks):
    c_ = CHUNK
    gr = DN_GROUP * c_
    ii = lax.broadcasted_iota(jnp.int32, (c_, c_), 0)
    jj = lax.broadcasted_iota(jnp.int32, (c_, c_), 1)
    tri_incl = ii >= jj
    tri_strict = ii > jj
    eye = ii == jj
    cw = cw_ref[...]
    heads = range(DN_HEADS)
    lanes = [slice(h * DN_HD, (h + 1) * DN_HD) for h in heads]

    def conv_silu(win, w4):
        delayed = jnp.dot(shift_ref[...], win, preferred_element_type=F32)
        y = (delayed[0:gr] * w4[0:1] + delayed[gr:2 * gr] * w4[1:2] + delayed[2 * gr:3 * gr] * w4[2:3]
             + win[gr:2 * gr].astype(F32) * w4[3:4])
        return y * _sigmoid(y)

    def window(ref, r0, first):
        if first:
            return jnp.concatenate([jnp.zeros((gr, DN_W), BF16), ref[0, 0:gr, :]], axis=0)
        return ref[0, pl.ds(pl.multiple_of(r0 - gr, gr), 2 * gr), :]

    def mm(a, b):
        return jnp.dot(a.astype(BF16), b.astype(BF16), preferred_element_type=F32)

    def gather(gi, first):
        r0 = gi * gr if isinstance(gi, int) else pl.multiple_of(gi * gr, gr)
        cq = conv_silu(window(dq_ref, r0, first), cw[:, 0:DN_W])
        ck = conv_silu(window(dk_ref, r0, first), cw[:, DN_W:2 * DN_W])
        cv = conv_silu(window(dv_ref, r0, first), cw[:, 2 * DN_W:3 * DN_W])
        ps = ps_ref[pl.ds(r0, gr), :]
        beta_all = _sigmoid(ps)
        g_all = -jnp.exp(acol_ref[...]) * _softplus(ps + dtcol_ref[...])
        g_hi, g_lo = _split_bf16(g_all)
        chains = []
        for j in range(DN_GROUP):
            cidx = gi * DN_GROUP + j
            rs = slice(j * c_, (j + 1) * c_)
            gc_all = jnp.dot(ltri_ref[...], jnp.concatenate([g_hi[rs], g_lo[rs]], axis=0),
                             preferred_element_type=F32)
            g_row = -jnp.exp(arow_ref[...]) * _softplus(rows_ref[0, cidx] + dtrow_ref[...])
            r_hi, r_lo = _split_bf16(g_row)
            gc_rows = jnp.dot(jnp.concatenate([r_hi, r_lo], axis=1), utri_ref[...],
                              preferred_element_type=F32)
            for h in heads:
                qh, kh, vh = cq[rs, lanes[h]], ck[rs, lanes[h]], cv[rs, lanes[h]]
                qn = qh * (lax.rsqrt(jnp.sum(qh * qh, axis=-1, keepdims=True) + EPS) * DN_HD ** -0.5)
                kn = kh * lax.rsqrt(jnp.sum(kh * kh, axis=-1, keepdims=True) + EPS)
                beta = beta_all[rs, h:h + 1]
                gcc = gc_all[:, 4 + h:5 + h]
                gcr = gc_rows[4 + h:5 + h, :]
                gl = gcr[:, c_ - 1:c_]
                chains.append(dict(
                    cidx=cidx, h=h, rows=pl.ds(r0 + j * c_, c_), qn=qn, kn=kn, kb=kn * beta, vb=vh * beta,
                    gcc=gcc, gl=gl, decay=jnp.exp(jnp.where(tri_incl, gcc - gcr, -1e30))))
        return chains

    def prep(si, first):
        chains = []
        for sg in range(DN_SUPER):
            chains += gather(si * DN_SUPER + sg, first and sg == 0)
        for ch in chains:
            ch["gram"] = lax.dot_general(jnp.concatenate([ch["kb"], ch["qn"]], axis=0).astype(BF16),
                                         ch["kn"].astype(BF16), NT, preferred_element_type=F32)
        for ch in chains:
            x = jnp.where(tri_strict, -(ch["gram"][:c_] * ch["decay"]), 0.0)
            a_scr[ch["cidx"], ch["h"]] = (ch["gram"][c_:] * ch["decay"]).astype(BF16)
            ch["acc"] = jnp.where(eye, 1.0, x)
            ch["x"] = x
        for ch in chains:
            ch["xp"] = mm(ch["x"], ch["x"])
        for _ in range(4):
            for ch in chains:
                ch["both"] = mm(jnp.concatenate([ch["acc"], ch["xp"]], axis=0), ch["xp"])
            for ch in chains:
                ch["acc"] = ch["acc"] + ch["both"][:c_]
                ch["xp"] = ch["both"][c_:]
        for ch in chains:
            ch["last"] = mm(ch["acc"], ch["xp"])
        for ch in chains:
            t_inv = ch["acc"] + ch["last"]
            gexp = jnp.exp(ch["gcc"])
            ch["gexp"] = gexp
            ch["uw"] = mm(t_inv, jnp.concatenate([ch["vb"], ch["kb"] * gexp], axis=1))
        for ch in chains:
            cidx, h, rows = ch["cidx"], ch["h"], ch["rows"]
            u_scr[rows, lanes[h]] = ch["uw"][:, :DN_HD]
            wq_scr[cidx, 0:c_, lanes[h]] = ch["uw"][:, DN_HD:].astype(BF16)
            wq_scr[cidx, c_:2 * c_, lanes[h]] = (ch["qn"] * ch["gexp"]).astype(BF16)
            kd_scr[rows, lanes[h]] = (ch["kn"] * jnp.exp(ch["gl"] - ch["gcc"])).astype(BF16)
            eg_scr[cidx, h] = jnp.broadcast_to(jnp.exp(ch["gl"]), (8, DN_HD))

    prep(0, True)

    def prep_body(gi, carry):
        prep(gi, False)
        return carry

    lax.fori_loop(1, n_chunks // (DN_GROUP * DN_SUPER), prep_body, 0)

    st_scr[...] = jnp.zeros_like(st_scr)

    def scan(ci, carry):
        states = [st_scr[h] for h in heads]
        for u in range(DN_GROUP):
            c = ci * DN_GROUP + u
            rows = pl.ds(pl.multiple_of(c * c_, c_), c_)
            wq = [jnp.dot(wq_scr[c, :, lanes[h]], states[h].astype(BF16), preferred_element_type=F32)
                  for h in heads]
            v_new = [(u_scr[rows, lanes[h]] - wq[h][:c_]).astype(BF16) for h in heads]
            kv = [lax.dot_general(kd_scr[rows, lanes[h]], v_new[h], TN, preferred_element_type=F32)
                  for h in heads]
            o = [wq[h][c_:] + jnp.dot(a_scr[c, h], v_new[h], preferred_element_type=F32) for h in heads]
            states = [states[h] * eg_scr[c, h][0:1, :] + kv[h] for h in heads]
            for h in heads:
                on = o[h] * lax.rsqrt(jnp.mean(o[h] * o[h], axis=-1, keepdims=True) + EPS) * norm_ref[...]
                z = dz_ref[0, rows, lanes[h]].astype(F32)
                o_ref[0, rows, lanes[h]] = (on * (z * _sigmoid(z))).astype(BF16)
        for h in heads:
            st_scr[h] = states[h]
        return carry

    lax.fori_loop(0, n_chunks // DN_GROUP, scan, 0)


def _deltanet(pm3, ps, rows4, cw, acol, dtcol, arow, dtrow, norm, ltri, utri, shift):
    b, s, _ = pm3.shape
    n_chunks = s // CHUNK
    assert n_chunks % (DN_GROUP * DN_SUPER) == 0
    base = 3 * ATT_W // DN_W
    full = lambda shape: pl.BlockSpec(shape, lambda bi: (0,) * len(shape))
    return pl.pallas_call(
        functools.partial(_dn_body, n_chunks=n_chunks),
        grid=(b,),
        in_specs=[
            pl.BlockSpec((1, s, DN_W), lambda bi: (bi, 0, base)),
            pl.BlockSpec((1, s, DN_W), lambda bi: (bi, 0, base + 1)),
            pl.BlockSpec((1, s, DN_W), lambda bi: (bi, 0, base + 2)),
            pl.BlockSpec((1, s, DN_W), lambda bi: (bi, 0, base + 3)),
            pl.BlockSpec((s, D_SMALL), lambda bi: (bi, 0)),
            pl.BlockSpec((1, n_chunks, 8, CHUNK), lambda bi: (bi, 0, 0, 0)),
            full((DN_CONV, 3 * DN_W)),
            full((1, D_SMALL)), full((1, D_SMALL)), full((8, 1)), full((8, 1)),
            full((1, DN_HD)),
            full((CHUNK, 2 * CHUNK)), full((2 * CHUNK, CHUNK)), full(shift.shape),
        ],
        out_specs=pl.BlockSpec((1, s, DN_W), lambda bi: (bi, 0, 0)),
        out_shape=jax.ShapeDtypeStruct((b, s, DN_W), BF16),
        scratch_shapes=[
            pltpu.VMEM((DN_HEADS, DN_HD, DN_HD), F32),
            pltpu.VMEM((s, DN_W), F32),
            pltpu.VMEM((n_chunks, 2 * CHUNK, DN_W), BF16),
            pltpu.VMEM((s, DN_W), BF16),
            pltpu.VMEM((n_chunks, DN_HEADS, CHUNK, CHUNK), BF16),
            pltpu.VMEM((n_chunks, DN_HEADS, 8, DN_HD), F32),
        ],
        compiler_params=pltpu.CompilerParams(
            dimension_semantics=("parallel",), vmem_limit_bytes=VMEM_LIMIT),
        name="deltanet",
    )(pm3, pm3, pm3, pm3, ps, rows4, cw, acol, dtcol, arow, dtrow, norm, ltri, utri, shift)


GLA_GROUP = 2
GLA_PREP = 4


def _gla_body(gq_ref, gk_ref, gv_ref, gz_ref, ps_ref, w2_ref, b2_ref, norm_ref, cmat_ref,
              o_ref, st_scr, att_scr, qe_scr, ke_scr, dec_scr, *, n_chunks):
    c_ = CHUNK
    rowi = lax.broadcasted_iota(jnp.int32, (c_, 1), 0)
    upper = [(rowi % (2 * s)) >= s for s in GLA_LEVELS]
    ii = lax.broadcasted_iota(jnp.int32, (2 * c_, c_), 0) % c_
    jj = lax.broadcasted_iota(jnp.int32, (2 * c_, c_), 1)
    same = [(ii // (2 * s)) == (jj // (2 * s)) for s in GLA_LEVELS]
    eye = ii == jj
    lane = lax.broadcasted_iota(jnp.int32, (c_, LANES), 1)
    first_head = lane < GLA_KD
    lane_sq = lax.broadcasted_iota(jnp.int32, (GLA_VD, LANES), 1)
    head_lanes = [lane_sq < GLA_KD, lane_sq >= GLA_KD]

    def stack_heads(t):
        return jnp.concatenate([jnp.where(first_head, t, 0.0), jnp.where(first_head, 0.0, t)], axis=0)

    pairs = range(GLA_HEADS // 2)
    pair_lanes = [slice(p * LANES, (p + 1) * LANES) for p in pairs]
    gr = GLA_PREP * c_

    def prep(gi, carry):
        r0 = pl.multiple_of(gi * gr, gr)
        pre = (jnp.dot(ps_ref[pl.ds(r0, gr), :].astype(BF16), w2_ref[...], preferred_element_type=F32)
               + b2_ref[...])
        gk = (jnp.minimum(pre, 0.0) - jnp.log(1.0 + jnp.exp(-jnp.abs(pre)))) * (1.0 / GLA_TAU)
        g_hi, g_lo = _split_bf16(gk)
        chains = []
        for j in range(GLA_PREP):
            rs = slice(j * c_, (j + 1) * c_)
            rows = pl.ds(r0 + j * c_, c_)
            br = jnp.dot(cmat_ref[...], jnp.concatenate([g_hi[rs], g_lo[rs]], axis=0),
                         preferred_element_type=F32)
            for p in pairs:
                q = gq_ref[0, rows, pair_lanes[p]].astype(F32) * (GLA_KD ** -0.5)
                k = gk_ref[0, rows, pair_lanes[p]].astype(F32)
                chains.append(dict(cidx=gi * GLA_PREP + j, p=p, rows=rows, br=br, q=q, k=k,
                                   bcum=br[0:c_, pair_lanes[p]]))
        for ch in chains:
            ch["part"] = lax.dot_general(stack_heads(ch["q"]).astype(BF16), ch["k"].astype(BF16), NT,
                                         preferred_element_type=F32)
        for ch in chains:
            ch["att"] = jnp.where(eye, ch["part"], 0.0)
        for li in range(len(GLA_LEVELS)):
            up = upper[li]
            for ch in chains:
                ref = ch["br"][c_ * (li + 1):c_ * (li + 2), pair_lanes[ch["p"]]]
                e = jnp.exp(jnp.where(up, ch["bcum"] - ref, ref - ch["bcum"]))
                qt = jnp.where(up, ch["q"] * e, 0.0)
                kt = jnp.where(up, 0.0, ch["k"] * e)
                ch["part"] = lax.dot_general(stack_heads(qt).astype(BF16), kt.astype(BF16), NT,
                                             preferred_element_type=F32)
            for ch in chains:
                ch["att"] = ch["att"] + jnp.where(same[li], ch["part"], 0.0)
        for ch in chains:
            cidx, p, rows, bcum = ch["cidx"], ch["p"], ch["rows"], ch["bcum"]
            blast = bcum[c_ - 1:c_, :]
            att_scr[cidx, 2 * p] = ch["att"][:c_].astype(BF16)
            att_scr[cidx, 2 * p + 1] = ch["att"][c_:].astype(BF16)
            qe_scr[rows, pair_lanes[p]] = (ch["q"] * jnp.exp(bcum)).astype(BF16)
            ke_scr[rows, pair_lanes[p]] = (ch["k"] * jnp.exp(blast - bcum)).astype(BF16)
            dec_scr[cidx, p] = jnp.broadcast_to(jnp.exp(blast), (8, LANES))
        return carry

    lax.fori_loop(0, n_chunks // GLA_PREP, prep, 0)

    st_scr[...] = jnp.zeros_like(st_scr)
    heads = range(GLA_HEADS)
    head_cols = [slice(h * GLA_VD, (h + 1) * GLA_VD) for h in heads]

    def scan(ci, carry):
        states = [st_scr[h] for h in heads]
        for u in range(GLA_GROUP):
            c = ci * GLA_GROUP + u
            rows = pl.ds(pl.multiple_of(c * c_, c_), c_)
            qe = [qe_scr[rows, pair_lanes[p]] for p in pairs]
            ke = [ke_scr[rows, pair_lanes[p]] for p in pairs]
            vh = [gv_ref[0, rows, head_cols[h]] for h in heads]
            upd = [lax.dot_general(vh[h], ke[h // 2], TN, preferred_element_type=F32) for h in heads]
            intra = [jnp.dot(att_scr[c, h], vh[h], preferred_element_type=F32) for h in heads]
            inter = [lax.dot_general(qe[h // 2], states[h].astype(BF16), NT, preferred_element_type=F32)
                     for h in heads]
            states = [states[h] * dec_scr[c, h // 2][0:1, :] + jnp.where(head_lanes[h % 2], upd[h], 0.0)
                      for h in heads]
            for h in heads:
                o = inter[h] + intra[h]
                o = o * lax.rsqrt(jnp.mean(o * o, axis=-1, keepdims=True) + EPS) * norm_ref[...]
                z = gz_ref[0, rows, head_cols[h]].astype(F32)
                o_ref[0, rows, head_cols[h]] = (o * (z * _sigmoid(z))).astype(BF16)
        for h in heads:
            st_scr[h] = states[h]
        return carry

    lax.fori_loop(0, n_chunks // GLA_GROUP, scan, 0)


def _gla(pm3, ps, w2, b2, norm, cmat):
    b, s, _ = pm3.shape
    n_chunks = s // CHUNK
    off = 3 * ATT_W + 4 * DN_W
    full = lambda shape: pl.BlockSpec(shape, lambda bi: (0,) * len(shape))
    return pl.pallas_call(
        functools.partial(_gla_body, n_chunks=n_chunks),
        grid=(b,),
        in_specs=[
            pl.BlockSpec((1, s, GLA_KW), lambda bi: (bi, 0, off // GLA_KW)),
            pl.BlockSpec((1, s, GLA_KW), lambda bi: (bi, 0, off // GLA_KW + 1)),
            pl.BlockSpec((1, s, GLA_W), lambda bi: (bi, 0, (off + 2 * GLA_KW) // GLA_W)),
            pl.BlockSpec((1, s, GLA_W), lambda bi: (bi, 0, (off + 2 * GLA_KW) // GLA_W + 1)),
            pl.BlockSpec((s, D_SMALL), lambda bi: (bi, 0)),
            full((D_SMALL, GLA_KW)), full((1, GLA_KW)), full((1, GLA_VD)),
            full(((len(GLA_LEVELS) + 1) * CHUNK, 2 * CHUNK)),
        ],
        out_specs=pl.BlockSpec((1, s, GLA_W), lambda bi: (bi, 0, 0)),
        out_shape=jax.ShapeDtypeStruct((b, s, GLA_W), BF16),
        scratch_shapes=[
            pltpu.VMEM((GLA_HEADS, GLA_VD, LANES), F32),
            pltpu.VMEM((n_chunks, GLA_HEADS, CHUNK, CHUNK), BF16),
            pltpu.VMEM((s, GLA_KW), BF16),
            pltpu.VMEM((s, GLA_KW), BF16),
            pltpu.VMEM((n_chunks, GLA_HEADS // 2, 8, LANES), F32),
        ],
        compiler_params=pltpu.CompilerParams(
            dimension_semantics=("parallel",), vmem_limit_bytes=VMEM_LIMIT),
        name="gla",
    )(pm3, pm3, pm3, pm3, ps, w2, b2, norm, cmat)


def _out_body(x_ref, ya_ref, yd_ref, yg_ref, p_ref, wo_ref, postg_ref, pg_ref, pp_ref, pn_ref, o_ref):
    y = (jnp.dot(ya_ref[...], wo_ref[0:ATT_W, :], preferred_element_type=F32)
         + jnp.dot(yd_ref[...], wo_ref[ATT_W:ATT_W + DN_W, :], preferred_element_type=F32)
         + jnp.dot(yg_ref[...], wo_ref[ATT_W + DN_W:D_MIX, :], preferred_element_type=F32))
    x1 = x_ref[...] + y * lax.rsqrt(jnp.mean(y * y, axis=-1, keepdims=True) + EPS) * postg_ref[...]
    gate = _sigmoid(jnp.dot(x1.astype(BF16), pg_ref[...], preferred_element_type=F32))
    pe = jnp.dot(p_ref[...].astype(BF16), pp_ref[...], preferred_element_type=F32) * gate
    o_ref[...] = x1 + pe * lax.rsqrt(jnp.mean(pe * pe, axis=-1, keepdims=True) + EPS) * pn_ref[...]


def _out(x2, ya, yd, yg, p3, wo, postg, pg, pp, pn, *, layer, tm):
    t = x2.shape[0]
    tile = lambda w: pl.BlockSpec((tm, w), lambda i: (i, 0))
    full = lambda shape: pl.BlockSpec(shape, lambda i: (0, 0))
    stacked = lambda shape: pl.BlockSpec((None,) + shape, lambda i: (layer, 0, 0), pipeline_mode=pl.Buffered(1))
    return pl.pallas_call(
        _out_body,
        grid=(t // tm,),
        in_specs=[tile(D_MODEL), tile(ATT_W), tile(DN_W), tile(GLA_W),
                  pl.BlockSpec((None, tm, D_PLE), lambda i: (layer, i, 0)),
                  stacked((D_MIX, D_MODEL)), full((1, D_MODEL)), stacked((D_MODEL, D_MODEL)),
                  stacked((D_PLE, D_MODEL)), full((1, D_MODEL))],
        out_specs=tile(D_MODEL),
        out_shape=jax.ShapeDtypeStruct((t, D_MODEL), F32),
        compiler_params=pltpu.CompilerParams(
            dimension_semantics=("parallel",), vmem_limit_bytes=VMEM_LIMIT),
        name="out",
    )(x2, ya, yd, yg, p3, wo, postg, pg, pp, pn)


def _chunk_constants():
    c = CHUNK
    ltri = np.tril(np.ones((c, c), np.float32))
    blocks = [ltri]
    idx = np.arange(c)
    for s in GLA_LEVELS:
        bnd = (idx // (2 * s)) * (2 * s) + s - 1
        blocks.append(ltri[bnd])
    cmat = np.concatenate(blocks, axis=0)
    gr = DN_GROUP * c
    shift = np.zeros((DN_CONV - 1, gr, 2 * gr), np.float32)
    t = np.arange(gr)
    for i in range(DN_CONV - 1):
        shift[i, t, gr + t - (DN_CONV - 1) + i] = 1.0
    return (jnp.asarray(np.concatenate([ltri, ltri], axis=1), BF16),
            jnp.asarray(np.concatenate([ltri.T, ltri.T], axis=0), BF16),
            jnp.asarray(np.concatenate([cmat, cmat], axis=1), BF16),
            jnp.asarray(shift.reshape((DN_CONV - 1) * gr, 2 * gr), BF16))


def _pick(n, pref):
    for t in pref:
        if n % t == 0:
            return t
    return n


def _prepare_weights(w_in, w_out, ple_proj, ple_gate):
    n_big = 4 * ATT_W + 4 * DN_W
    n_small = 2 * DN_HEADS
    wt = jnp.swapaxes(w_in, 1, 2)
    wm = jnp.concatenate([wt[:, :2 * ATT_W], wt[:, 3 * ATT_W:n_big],
                          wt[:, n_big + n_small:n_big + n_small + 2 * GLA_KW + 2 * GLA_W]],
                         axis=1).astype(BF16)
    wvt = wt[:, 2 * ATT_W:3 * ATT_W].astype(BF16)
    wst = wt[:, n_big:n_big + n_small].astype(BF16)
    w_small = jnp.concatenate([w_in[:, :, n_big:n_big + n_small], w_in[:, :, -GLA_RANK:]], axis=2)
    ws = jnp.pad(w_small, ((0, 0), (0, 0), (0, D_SMALL - w_small.shape[2]))).astype(BF16)
    return wm, ws, wst, wvt, w_out.astype(BF16), ple_gate.astype(BF16), ple_proj.astype(BF16)


def _layer(x, p3, weights, pre_g, post_g, lq1, lk1, lq2, lk2, att_subln, dn_conv, dn_a_log,
           dn_dt_bias, dn_norm, gla_w2, gla_b, gla_norm, ple_norm, layer_idx, consts):
    b, s, _ = x.shape
    t = b * s
    ltri, utri, cmat, shift = consts
    wm, ws, wst, wvt, wo, pg, pp = weights
    lam_init = 0.8 - 0.6 * math.exp(-0.3 * layer_idx)
    n_small = 2 * DN_HEADS

    x2 = x.reshape(t, D_MODEL)
    pm, ps, pst, vt = _proj(x2, pre_g.reshape(1, D_MODEL), wm, ws, wst, wvt, layer=layer_idx, b=b, s=s,
                            tm=_pick(s, (1024, 512, 256)), tn=_pick(D_MAIN, (3328, 512)))
    pm3 = pm.reshape(b, s, D_MAIN)

    lqk = jnp.stack([lq1, lk1, lq2, lk2], axis=0)
    y_att = _attn(pm3, vt, lqk, att_subln.reshape(1, ATT_VD), tq=LANES, lam_init=lam_init)

    rows4 = pst.reshape(8, b, s // CHUNK, CHUNK).transpose(1, 2, 0, 3)
    lane_vec = lambda v, off: jnp.zeros((1, D_SMALL), F32).at[0, off:off + v.shape[0]].set(v)
    row_vec = lambda v, off: jnp.zeros((8, 1), F32).at[off:off + v.shape[0], 0].set(v)
    y_dn = _deltanet(pm3, ps, rows4, dn_conv, lane_vec(dn_a_log, DN_HEADS), lane_vec(dn_dt_bias, DN_HEADS),
                     row_vec(dn_a_log, DN_HEADS), row_vec(dn_dt_bias, DN_HEADS),
                     dn_norm.reshape(1, DN_HD), ltri, utri, shift)

    w2 = jnp.zeros((D_SMALL, GLA_KW), F32).at[n_small:n_small + GLA_RANK].set(gla_w2).astype(BF16)
    y_gla = _gla(pm3, ps, w2, gla_b.reshape(1, GLA_KW), gla_norm.reshape(1, GLA_VD), cmat)

    out = _out(x2, y_att.reshape(t, ATT_W), y_dn.reshape(t, DN_W), y_gla.reshape(t, GLA_W), p3,
               wo, post_g.reshape(1, D_MODEL), pg, pp, ple_norm.reshape(1, D_MODEL),
               layer=layer_idx, tm=_pick(t, (1024, 512, 256)))
    return out.reshape(b, s, D_MODEL)


def kernel(x, p, w_in, w_out, pre_gain, post_gain, att_lq1, att_lk1, att_lq2, att_lk2, att_subln, dn_conv, dn_a_log, dn_dt_bias, dn_norm, gla_w2, gla_b, gla_norm, ple_proj, ple_gate, ple_norm):
    consts = _chunk_constants()
    weights = _prepare_weights(w_in, w_out, ple_proj, ple_gate)
    p3 = p.reshape(p.shape[0], -1, D_PLE)
    for i in range(p.shape[0]):
        x = _layer(x, p3, weights, pre_gain[i], post_gain[i], att_lq1[i], att_lk1[i],
                   att_lq2[i], att_lk2[i], att_subln[i], dn_conv[i], dn_a_log[i], dn_dt_bias[i],
                   dn_norm[i], gla_w2[i], gla_b[i], gla_norm[i], ple_norm[i], i, consts)
    return x
```
